```python
import math
import jax
import jax.numpy as jnp
from jax import lax
import numpy as np

D_MODEL = 1024
BATCH = 2
SEQ = 8192
DEPTH = 4
DEC_BATCH = 2
DEC_SEQ = 16384
PAST_LEN = 128

N_MEM = 256
DIL_PAIRS = ((128, 1), (512, 4), (2048, 16))
N_DIL_GROUPS = len(DIL_PAIRS)
HEADS_PER_GROUP = 4
N_HEADS_A = N_DIL_GROUPS * HEADS_PER_GROUP
HEAD_DIM_A = 128
QKV_W = N_HEADS_A * HEAD_DIM_A
ATTN_OUT_W = HEADS_PER_GROUP * HEAD_DIM_A
REL_BUCKETS = 32
REL_MAX_EXACT = 8
REL_MAX_DIST = 1024
POOL_SIZES = (2, 4, 8, 16)
N_POOL_GROUPS = len(POOL_SIZES)
POOL_GROUP = 128
POOL_W = N_POOL_GROUPS * POOL_GROUP
CONV_W = 512
CONV_K = 31
N_BRANCH = 3
X_HEADS = 4
X_HEAD_DIM = 128
X_W = X_HEADS * X_HEAD_DIM
D_FF = 2816
IN_W = 3 * QKV_W + POOL_W + 2 * CONV_W + N_BRANCH * D_MODEL
IN_SPLITS = (QKV_W, 2 * QKV_W, 3 * QKV_W, 3 * QKV_W + POOL_W, 3 * QKV_W + POOL_W + 2 * CONV_W)
EPS = 1e-6
NEG_INF = -1e30

kernel_name = 'hybrid_dilated_pool_conv_encoder'


def rms_norm(x, g):
    xf = x.astype(jnp.float32)
    y = xf * lax.rsqrt(jnp.mean(xf * xf, axis=-1, keepdims=True) + EPS)
    return (y * g.astype(jnp.float32)).astype(x.dtype)


def layer_norm(x, g, b):
    xf = x.astype(jnp.float32)
    mu = jnp.mean(xf, axis=-1, keepdims=True)
    xc = xf - mu
    var = jnp.mean(xc * xc, axis=-1, keepdims=True)
    y = xc * lax.rsqrt(var + EPS) * g.astype(jnp.float32) + b.astype(jnp.float32)
    return y.astype(x.dtype)


def swiglu_ffn(x, norm_g, w_gu, w_down):
    h = rms_norm(x, norm_g) @ w_gu
    a, u = jnp.split(h, 2, axis=-1)
    return (jax.nn.silu(a) * u) @ w_down


def t5_bucket(rel):
    half = REL_BUCKETS // 2
    n = jnp.abs(rel)
    nf = jnp.maximum(n, 1).astype(jnp.float32)
    large = REL_MAX_EXACT + (jnp.log(nf / REL_MAX_EXACT) / math.log(REL_MAX_DIST / REL_MAX_EXACT)
                             * (half - REL_MAX_EXACT)).astype(jnp.int32)
    large = jnp.minimum(large, half - 1)
    return jnp.where(rel > 0, half, 0) + jnp.where(n < REL_MAX_EXACT, n, large)


def dilated_band_attention(q, k, v, r, half, bias):
    b, s, h, dh = q.shape
    L = s // r
    blk = half
    nblk = -(-L // blk)
    lp = nblk * blk

    def to_sub(t):
        return t.reshape(b, L, r, h, dh).transpose(0, 2, 1, 3, 4).reshape(b * r, L, h, dh)

    def band(t):
        tp = jnp.pad(t, ((0, 0), (blk, lp - L + blk), (0, 0), (0, 0))).reshape(b * r, nblk + 2, blk, h, dh)
        return jnp.concatenate([tp[:, :-2], tp[:, 1:-1], tp[:, 2:]], axis=2)

    qs = to_sub(q)
    qb = jnp.pad(qs, ((0, 0), (0, lp - L), (0, 0), (0, 0))).reshape(b * r, nblk, blk, h, dh)
    kb = band(to_sub(k))
    vb = band(to_sub(v))
    scores = jnp.einsum('bnqhd,bnkhd->bnhqk', qb, kb, preferred_element_type=jnp.float32) * (dh ** -0.5)
    scores = scores + bias[None, None]
    q_idx = jnp.arange(blk)[:, None]
    k_idx = jnp.arange(3 * blk)[None, :]
    offset = k_idx - blk - q_idx
    key_sub = jnp.arange(nblk)[:, None, None] * blk - blk + k_idx[None]
    valid = (jnp.abs(offset) <= half)[None] & (key_sub >= 0) & (key_sub < L)
    scores = jnp.where(valid[None, :, None], scores, NEG_INF)
    lse = jax.nn.logsumexp(scores, axis=-1)
    p = jnp.exp(scores - lse[..., None])
    o = jnp.einsum('bnhqk,bnkhd->bnqhd', p.astype(vb.dtype), vb)
    o = o.reshape(b * r, lp, h, dh)[:, :L]
    lse = lse.transpose(0, 1, 3, 2).reshape(b * r, lp, h)[:, :L]

    def from_sub(t):
        return t.reshape(b, r, L, *t.shape[2:]).swapaxes(1, 2).reshape(b, s, *t.shape[2:])

    return from_sub(o), from_sub(lse)


def dilated_attention_mixer(q, k, v, rel_bias):
    b, s = q.shape[:2]
    outs, lses = [], []
    for g, (w, r) in enumerate(DIL_PAIRS):
        half = w // (2 * r)
        hs = slice(g * HEADS_PER_GROUP, (g + 1) * HEADS_PER_GROUP)
        offset = jnp.arange(3 * half)[None, :] - half - jnp.arange(half)[:, None]
        bias = rel_bias[t5_bucket(r * offset)][..., hs].transpose(2, 0, 1).astype(jnp.float32)
        o, lse = dilated_band_attention(q[:, :, hs], k[:, :, hs], v[:, :, hs], r, half, bias)
        outs.append(o)
        lses.append(lse)
    o = jnp.stack(outs, axis=2)
    alpha = jax.nn.softmax(jnp.stack(lses, axis=2), axis=2)
    out = jnp.einsum('bsgh,bsghd->bshd', alpha.astype(o.dtype), o)
    return out.reshape(b, s, ATTN_OUT_W)


def pooling_mixer(xp, pool_w, pool_scale):
    b, s, _ = xp.shape
    xf = xp.astype(jnp.float32).reshape(b, s, N_POOL_GROUPS, POOL_GROUP)
    cs = jnp.pad(lax.cumsum(xf, axis=1), ((0, 0), (1, 0), (0, 0), (0, 0)))
    pos = jnp.arange(s)
    outs = []
    for g, kw in enumerate(POOL_SIZES):
        lo = jnp.maximum(pos - kw // 2, 0)
        hi = jnp.minimum(pos + kw // 2, s)
        cnt = (hi - lo).astype(jnp.float32)
        mean = (cs[:, hi, g] - cs[:, lo, g]) / cnt[None, :, None]
        outs.append(mean - xf[:, :, g])
    pooled = jnp.stack(outs, axis=2).astype(xp.dtype)
    mixed = jnp.einsum('bsgc,gcd->bsgd', pooled, pool_w).reshape(b, s, POOL_W)
    return mixed * pool_scale


def conv_module(xc, conv_dw, conv_b, ln_g, ln_b):
    a, gt = jnp.split(xc, 2, axis=-1)
    u = a * jax.nn.sigmoid(gt)
    u = lax.conv_general_dilated(u, conv_dw[:, None, :].astype(u.dtype), window_strides=(1,),
                                 padding=[(CONV_K // 2, CONV_K // 2)],
                                 dimension_numbers=('NWC', 'WIO', 'NWC'),
                                 feature_group_count=CONV_W) + conv_b
    return jax.nn.silu(layer_norm(u, ln_g, ln_b))


def memory_cross_attention(x, mem, xn_g, mn_g, wq, wkv, wo):
    b, s, _ = x.shape
    m = mem.shape[1]
    q = (rms_norm(x, xn_g) @ wq).reshape(b, s, X_HEADS, X_HEAD_DIM)
    k, v = jnp.split(rms_norm(mem, mn_g) @ wkv, 2, axis=-1)
    k = k.reshape(b, m, X_HEADS, X_HEAD_DIM)
    v = v.reshape(b, m, X_HEADS, X_HEAD_DIM)
    scores = jnp.einsum('bshd,bmhd->bhsm', q, k, preferred_element_type=jnp.float32) * (X_HEAD_DIM ** -0.5)
    p = jax.nn.softmax(scores, axis=-1)
    o = jnp.einsum('bhsm,bmhd->bshd', p.astype(v.dtype), v).reshape(b, s, X_W)
    return o @ wo


def encoder_trunk(x, mem, weights):
    (rel_bias, ffn1_norm, ffn1_w_gu, ffn1_w_down, mix_norm, w_in, pool_w, pool_scale,
     conv_dw, conv_b, conv_ln_g, conv_ln_b, w_br_attn, w_br_pool, w_br_conv, w_out,
     xattn_norm, mem_norm, xattn_wq, xattn_wkv, xattn_wo,
     ffn2_norm, ffn2_w_gu, ffn2_w_down, final_norm) = weights
    b, s, d = x.shape
    for l in range(DEPTH):
        x = x + 0.5 * swiglu_ffn(x, ffn1_norm[l], ffn1_w_gu[l], ffn1_w_down[l])
        u = rms_norm(x, mix_norm[l]) @ w_in[l]
        q, k, v, xp, xc, gates = jnp.split(u, IN_SPLITS, axis=-1)
        q = q.reshape(b, s, N_HEADS_A, HEAD_DIM_A)
        k = k.reshape(b, s, N_HEADS_A, HEAD_DIM_A)
        v = v.reshape(b, s, N_HEADS_A, HEAD_DIM_A)
        ya = dilated_attention_mixer(q, k, v, rel_bias) @ w_br_attn[l]
        yb = pooling_mixer(xp, pool_w[l], pool_scale[l]) @ w_br_pool[l]
        yc = conv_module(xc, conv_dw[l], conv_b[l], conv_ln_g[l], conv_ln_b[l]) @ w_br_conv[l]
        g = jax.nn.sigmoid(gates.reshape(b, s, N_BRANCH, d))
        merged = g[:, :, 0] * ya + g[:, :, 1] * yb + g[:, :, 2] * yc
        x = x + merged @ w_out[l]
        x = x + memory_cross_attention(x, mem, xattn_norm[l], mem_norm[l], xattn_wq[l], xattn_wkv[l], xattn_wo[l])
        x = x + 0.5 * swiglu_ffn(x, ffn2_norm[l], ffn2_w_gu[l], ffn2_w_down[l])
    return rms_norm(x, final_norm)


def _dense(key, shape, fan_in):
    return jax.random.normal(key, shape, jnp.float32) * (fan_in ** -0.5)


def _gain(key, shape):
    return 1.0 + 0.05 * jax.random.normal(key, shape, jnp.float32)


def _small(key, shape):
    return 0.02 * jax.random.normal(key, shape, jnp.float32)


def setup_inputs(seed: int = 0) -> dict:
    key = jax.random.key(seed)
    ks = jax.random.split(key, 29)
    L, D = DEPTH, D_MODEL
    return {
        'x_prompt': jax.random.normal(ks[0], (BATCH, SEQ, D), jnp.float32),
        'x_sample': jax.random.normal(ks[1], (DEC_BATCH, DEC_SEQ, D), jnp.float32),
        'mem_prompt': jax.random.normal(ks[2], (BATCH, N_MEM, D), jnp.float32),
        'mem_sample': jax.random.normal(ks[3], (DEC_BATCH, N_MEM, D), jnp.float32),
        'rel_bias': 0.2 * jax.random.normal(ks[4], (REL_BUCKETS, N_HEADS_A), jnp.float32),
        'ffn1_norm': _gain(ks[5], (L, D)),
        'ffn1_w_gu': _dense(ks[6], (L, D, 2 * D_FF), D),
        'ffn1_w_down': _dense(ks[7], (L, D_FF, D), D_FF),
        'mix_norm': _gain(ks[8], (L, D)),
        'w_in': _dense(ks[9], (L, D, IN_W), D),
        'pool_w': _dense(ks[10], (L, N_POOL_GROUPS, POOL_GROUP, POOL_GROUP), POOL_GROUP),
        'pool_scale': _gain(ks[11], (L, POOL_W)),
        'conv_dw': _dense(ks[12], (L, CONV_K, CONV_W), CONV_K),
        'conv_b': _small(ks[13], (L, CONV_W)),
        'conv_ln_g': _gain(ks[14], (L, CONV_W)),
        'conv_ln_b': _small(ks[15], (L, CONV_W)),
        'w_br_attn': _dense(ks[16], (L, ATTN_OUT_W, D), ATTN_OUT_W),
        'w_br_pool': _dense(ks[17], (L, POOL_W, D), POOL_W),
        'w_br_conv': _dense(ks[18], (L, CONV_W, D), CONV_W),
        'w_out': _dense(ks[19], (L, D, D), D),
        'xattn_norm': _gain(ks[20], (L, D)),
        'mem_norm': _gain(ks[21], (L, D)),
        'xattn_wq': _dense(ks[22], (L, D, X_W), D),
        'xattn_wkv': _dense(ks[23], (L, D, 2 * X_W), D),
        'xattn_wo': _dense(ks[24], (L, X_W, D), X_W),
        'ffn2_norm': _gain(ks[25], (L, D)),
        'ffn2_w_gu': _dense(ks[26], (L, D, 2 * D_FF), D),
        'ffn2_w_down': _dense(ks[27], (L, D_FF, D), D_FF),
        'final_norm': _gain(ks[28], (D,)),
    }


def reference(x_prompt, x_sample, mem_prompt, mem_sample, rel_bias, ffn1_norm, ffn1_w_gu, ffn1_w_down,
              mix_norm, w_in, pool_w, pool_scale, conv_dw, conv_b, conv_ln_g, conv_ln_b,
              w_br_attn, w_br_pool, w_br_conv, w_out, xattn_norm, mem_norm, xattn_wq, xattn_wkv, xattn_wo,
              ffn2_norm, ffn2_w_gu, ffn2_w_down, final_norm):
    weights = (rel_bias, ffn1_norm, ffn1_w_gu, ffn1_w_down, mix_norm, w_in, pool_w, pool_scale,
               conv_dw, conv_b, conv_ln_g, conv_ln_b, w_br_attn, w_br_pool, w_br_conv, w_out,
               xattn_norm, mem_norm, xattn_wq, xattn_wkv, xattn_wo,
               ffn2_norm, ffn2_w_gu, ffn2_w_down, final_norm)
    y_prompt = encoder_trunk(x_prompt, mem_prompt, weights)
    y_sample = encoder_trunk(x_sample, mem_sample, weights)
    return (y_prompt, y_sample)
```

```python
import functools
import math

import jax
import jax.numpy as jnp
from jax import lax
from jax.experimental import pallas as pl
from jax.experimental.pallas import tpu as pltpu

F32 = jnp.float32
BF16 = jnp.bfloat16

D_MODEL = 1024
D_FF = 2816
DIL_PAIRS = ((128, 1), (512, 4), (2048, 16))
HEADS_PER_GROUP = 4
N_HEADS_A = len(DIL_PAIRS) * HEADS_PER_GROUP
HEAD_DIM = 128
QKV_W = N_HEADS_A * HEAD_DIM
ATTN_W = HEADS_PER_GROUP * HEAD_DIM
BAND_HALF = 64
REL_BUCKETS = 32
REL_MAX_EXACT = 8
REL_MAX_DIST = 1024
POOL_SIZES = (2, 4, 8, 16)
POOL_GROUP = 128
POOL_W = len(POOL_SIZES) * POOL_GROUP
CONV_W = 512
CONV_K = 31
X_HEADS = 4
X_W = X_HEADS * HEAD_DIM
PROJ_W = 3 * QKV_W + POOL_W + 2 * CONV_W
EPS = 1e-6
NEG_INF = -1e30

HALO = 16
Q_SUB = 128
VMEM_LIMIT = 56 * 1024 * 1024


def _params(n_axes):
    return pltpu.CompilerParams(dimension_semantics=("arbitrary",) * n_axes,
                                vmem_limit_bytes=VMEM_LIMIT)


def _resident(shape, index):
    return pl.BlockSpec(shape, lambda *_: index, pipeline_mode=pl.Buffered(1))


def _rms(xf, g):
    ms = jnp.mean(xf * xf, axis=-1, keepdims=True)
    return xf * lax.rsqrt(ms + EPS) * g


def _sigmoid(x):
    return 1.0 / (1.0 + jnp.exp(-x))


def _ffn_body(x_ref, g_ref, wgu_ref, wd_ref, fg_ref, o_ref, xn_ref, act_ref, *, ck, final):
    xf = x_ref[0]
    xn_ref[...] = _rms(xf, g_ref[0]).astype(BF16)
    for c in range(D_FF // ck):
        xn = xn_ref[...]
        a = jnp.dot(xn, wgu_ref[0, :, c * ck:(c + 1) * ck], preferred_element_type=F32)
        u = jnp.dot(xn, wgu_ref[0, :, D_FF + c * ck:D_FF + (c + 1) * ck], preferred_element_type=F32)
        act_ref[:, c * ck:(c + 1) * ck] = (a * _sigmoid(a) * u).astype(BF16)
    y = xf + 0.5 * jnp.dot(act_ref[...], wd_ref[0], preferred_element_type=F32)
    if final:
        y = _rms(y, fg_ref[...])
    o_ref[0] = y


def _ffn(x, norm, w_gu, w_down, final_g, layer, *, final, tm=512, ck=256):
    b, s, d = x.shape
    tm = min(tm, s)
    body = functools.partial(_ffn_body, ck=ck, final=final)
    return pl.pallas_call(
        body,
        out_shape=jax.ShapeDtypeStruct(x.shape, F32),
        grid=(b, s // tm),
        in_specs=[
            pl.BlockSpec((1, tm, d), lambda i, j: (i, j, 0)),
            _resident((1, 1, d), (layer, 0, 0)),
            _resident((1, d, 2 * D_FF), (layer, 0, 0)),
            _resident((1, D_FF, d), (layer, 0, 0)),
            _resident((1, d), (0, 0)),
        ],
        out_specs=pl.BlockSpec((1, tm, d), lambda i, j: (i, j, 0)),
        scratch_shapes=[pltpu.VMEM((tm, d), BF16), pltpu.VMEM((tm, D_FF), BF16)],
        compiler_params=_params(2),
        name="ffn",
    )(x, norm, w_gu, w_down, final_g)


def _proj_body(x_ref, g_ref, w_ref, q_ref, k_ref, v_ref, xp_ref, xc_ref, xn_ref, *, cw):
    xn_ref[...] = _rms(x_ref[0], g_ref[0]).astype(BF16)
    outs = ((q_ref, 0, QKV_W), (k_ref, QKV_W, QKV_W), (v_ref, 2 * QKV_W, QKV_W),
            (xp_ref, 3 * QKV_W, POOL_W), (xc_ref, 3 * QKV_W + POOL_W, 2 * CONV_W))
    for ref, base, width in outs:
        for c in range(width // cw):
            y = jnp.dot(xn_ref[...], w_ref[0, :, base + c * cw:base + (c + 1) * cw],
                        preferred_element_type=F32)
            ref[0, :, c * cw:(c + 1) * cw] = y.astype(ref.dtype)


def _proj(x, norm, w_in, layer, *, tm=512, cw=512):
    b, s, d = x.shape
    tm = min(tm, s)
    row = lambda w: pl.BlockSpec((1, tm, w), lambda i, j: (i, j, 0))
    return pl.pallas_call(
        functools.partial(_proj_body, cw=cw),
        out_shape=(jax.ShapeDtypeStruct((b, s, QKV_W), BF16),
                   jax.ShapeDtypeStruct((b, s, QKV_W), BF16),
                   jax.ShapeDtypeStruct((b, s, QKV_W), BF16),
                   jax.ShapeDtypeStruct((b, s, POOL_W), F32),
                   jax.ShapeDtypeStruct((b, s, 2 * CONV_W), F32)),
        grid=(b, s // tm),
        in_specs=[row(d), _resident((1, 1, d), (layer, 0, 0)),
                  _resident((1, d, PROJ_W), (layer, 0, 0))],
        out_specs=(row(QKV_W), row(QKV_W), row(QKV_W), row(POOL_W), row(2 * CONV_W)),
        scratch_shapes=[pltpu.VMEM((tm, d), BF16)],
        compiler_params=_params(2),
        name="in_proj",
    )(x, norm, w_in)


def _attn_body(q_ref, kp_ref, kc_ref, kn_ref, vp_ref, vc_ref, vn_ref, bias_ref,
               o_ref, lse_ref, kbuf, vbuf, *, tl, seq_l):
    h64 = BAND_HALF
    kbuf[0:h64] = kp_ref[0]
    kbuf[h64:h64 + tl] = kc_ref[0]
    kbuf[h64 + tl:h64 + tl + h64] = kn_ref[0]
    vbuf[0:h64] = vp_ref[0]
    vbuf[h64:h64 + tl] = vc_ref[0]
    vbuf[h64 + tl:h64 + tl + h64] = vn_ref[0]
    row0 = pl.program_id(2) * tl
    scale = HEAD_DIM ** -0.5
    lane = lax.broadcasted_iota(jnp.int32, (Q_SUB, HEAD_DIM), 1)
    for j in range(tl // Q_SUB):
        kpos = row0 + (j * Q_SUB - h64) + lax.broadcasted_iota(jnp.int32, (1, 2 * Q_SUB), 1)
        kmask = jnp.where((kpos >= 0) & (kpos < seq_l), 0.0, NEG_INF).astype(F32)
        lse_blk = jnp.zeros((Q_SUB, HEAD_DIM), F32)
        for h in range(HEADS_PER_GROUP):
            cols = slice(h * HEAD_DIM, (h + 1) * HEAD_DIM)
            q = q_ref[0, j * Q_SUB:(j + 1) * Q_SUB, cols]
            k = kbuf[j * Q_SUB:(j + 2) * Q_SUB, cols]
            v = vbuf[j * Q_SUB:(j + 2) * Q_SUB, cols]
            s = lax.dot_general(q, k, (((1,), (1,)), ((), ())), preferred_element_type=F32)
            s = s * scale + bias_ref[h] + kmask
            mx = jnp.max(s, axis=-1, keepdims=True)
            p = jnp.exp(s - mx)
            l = jnp.sum(p, axis=-1, keepdims=True)
            o = jnp.dot(p.astype(BF16), v, preferred_element_type=F32) / l
            o_ref[0, j * Q_SUB:(j + 1) * Q_SUB, cols] = o.astype(o_ref.dtype)
            lse_blk = jnp.where(lane == h, mx + jnp.log(l), lse_blk)
        lse_ref[0, j * Q_SUB:(j + 1) * Q_SUB, :] = lse_blk


def _attn_group(q, k, v, bias, g, r):
    b, s, _ = q.shape
    sl = s // r
    tl = min(512, sl)
    nh = sl // BAND_HALF
    per = tl // BAND_HALF
    view = lambda t: t.reshape(b, sl, r * QKV_W)
    col = lambda c: c * (QKV_W // ATTN_W) + g
    cur = pl.BlockSpec((1, tl, ATTN_W), lambda i, c, m: (i, m, col(c)))
    prev = pl.BlockSpec((1, BAND_HALF, ATTN_W), lambda i, c, m: (i, jnp.maximum(m * per - 1, 0), col(c)))
    nxt = pl.BlockSpec((1, BAND_HALF, ATTN_W), lambda i, c, m: (i, jnp.minimum((m + 1) * per, nh - 1), col(c)))
    qv, kv, vv = view(q), view(k), view(v)
    o, lse = pl.pallas_call(
        functools.partial(_attn_body, tl=tl, seq_l=sl),
        out_shape=(jax.ShapeDtypeStruct((b, sl, r * ATTN_W), BF16),
                   jax.ShapeDtypeStruct((b, sl, r * HEAD_DIM), F32)),
        grid=(b, r, sl // tl),
        in_specs=[cur, prev, cur, nxt, prev, cur, nxt,
                  _resident((HEADS_PER_GROUP, Q_SUB, 2 * Q_SUB), (0, 0, 0))],
        out_specs=(pl.BlockSpec((1, tl, ATTN_W), lambda i, c, m: (i, m, c)),
                   pl.BlockSpec((1, tl, HEAD_DIM), lambda i, c, m: (i, m, c))),
        scratch_shapes=[pltpu.VMEM((tl + 2 * BAND_HALF, ATTN_W), BF16),
                        pltpu.VMEM((tl + 2 * BAND_HALF, ATTN_W), BF16)],
        compiler_params=_params(3),
        name=f"dilated_attn_r{r}",
    )(qv, kv, kv, kv, vv, vv, vv, bias)
    return o.reshape(b, s, ATTN_W), lse.reshape(b, s, HEAD_DIM)


def _t5_bucket(rel):
    half = REL_BUCKETS // 2
    n = jnp.abs(rel)
    nf = jnp.maximum(n, 1).astype(F32)
    large = REL_MAX_EXACT + (jnp.log(nf / REL_MAX_EXACT) / math.log(REL_MAX_DIST / REL_MAX_EXACT)
                             * (half - REL_MAX_EXACT)).astype(jnp.int32)
    large = jnp.minimum(large, half - 1)
    return jnp.where(rel > 0, half, 0) + jnp.where(n < REL_MAX_EXACT, n, large)


def _band_bias(rel_bias, g, r):
    delta = jnp.arange(2 * Q_SUB)[None, :] - BAND_HALF - jnp.arange(Q_SUB)[:, None]
    hs = slice(g * HEADS_PER_GROUP, (g + 1) * HEADS_PER_GROUP)
    tbl = rel_bias[_t5_bucket(r * delta)][..., hs].transpose(2, 0, 1).astype(F32)
    return jnp.where((jnp.abs(delta) <= BAND_HALF)[None], tbl, NEG_INF)


def _mix_body(xp_p, xp_c, xp_n, xc_p, xc_c, xc_n, pw_ref, ps_ref, dw_ref, cb_ref, lg_ref, lb_ref,
              yb_ref, yc_ref, pbuf, ubuf, *, ts, seq, rows):
    i = pl.program_id(1)
    has_prev = (i > 0).astype(F32)
    has_next = (i < pl.num_programs(1) - 1).astype(F32)

    pbuf[0:HALO] = xp_p[0] * has_prev
    pbuf[HALO:HALO + ts] = xp_c[0]
    pbuf[HALO + ts:HALO + ts + HALO] = xp_n[0] * has_next
    pos = i * ts + lax.broadcasted_iota(jnp.int32, (ts, 1), 0)
    for g, kw in enumerate(POOL_SIZES):
        cols = slice(g * POOL_GROUP, (g + 1) * POOL_GROUP)
        acc = pbuf[HALO - kw // 2:HALO - kw // 2 + ts, cols]
        for dlt in range(-kw // 2 + 1, kw // 2):
            acc = acc + pbuf[HALO + dlt:HALO + dlt + ts, cols]
        cnt = (jnp.minimum(pos + kw // 2, seq) - jnp.maximum(pos - kw // 2, 0)).astype(F32)
        pooled = (acc / cnt - xp_c[0, :, cols]).astype(BF16)
        mixed = jnp.dot(pooled, pw_ref[0, g], preferred_element_type=F32)
        yb_ref[0, :, cols] = (mixed * ps_ref[0, :, cols]).astype(yb_ref.dtype)

    def glu(t):
        return t[:, :CONV_W] * _sigmoid(t[:, CONV_W:])
    ubuf[0:HALO] = glu(xc_p[0]) * has_prev
    ubuf[HALO:HALO + ts] = glu(xc_c[0])
    ubuf[HALO + ts:HALO + ts + HALO] = glu(xc_n[0]) * has_next
    base = HALO - CONV_K // 2
    for c in range(ts // rows):
        acc = jnp.zeros((rows, CONV_W), F32) + cb_ref[0]
        for kk in range(CONV_K):
            acc = acc + dw_ref[0, kk:kk + 1, :] * ubuf[base + c * rows + kk:base + c * rows + kk + rows, :]
        mu = jnp.mean(acc, axis=-1, keepdims=True)
        xc = acc - mu
        var = jnp.mean(xc * xc, axis=-1, keepdims=True)
        y = xc * lax.rsqrt(var + EPS) * lg_ref[0] + lb_ref[0]
        yc_ref[0, c * rows:(c + 1) * rows, :] = (y * _sigmoid(y)).astype(yc_ref.dtype)


def _mixers(xp, xc, pool_w, pool_scale, conv_dw, conv_b, ln_g, ln_b, layer, *, ts=256, rows=32):
    b, s, _ = xp.shape
    ts = min(ts, s)
    per = ts // HALO
    nh = s // HALO

    def halo_specs(w):
        return (pl.BlockSpec((1, HALO, w), lambda i, j: (i, jnp.maximum(j * per - 1, 0), 0)),
                pl.BlockSpec((1, ts, w), lambda i, j: (i, j, 0)),
                pl.BlockSpec((1, HALO, w), lambda i, j: (i, jnp.minimum((j + 1) * per, nh - 1), 0)))

    vec = lambda w: _resident((1, 1, w), (layer, 0, 0))
    out = pl.BlockSpec((1, ts, POOL_W), lambda i, j: (i, j, 0))
    return pl.pallas_call(
        functools.partial(_mix_body, ts=ts, seq=s, rows=rows),
        out_shape=(jax.ShapeDtypeStruct((b, s, POOL_W), BF16),
                   jax.ShapeDtypeStruct((b, s, CONV_W), BF16)),
        grid=(b, s // ts),
        in_specs=[*halo_specs(POOL_W), *halo_specs(2 * CONV_W),
                  _resident((1, len(POOL_SIZES), POOL_GROUP, POOL_GROUP), (layer, 0, 0, 0)),
                  vec(POOL_W),
                  _resident((1, CONV_K, CONV_W), (layer, 0, 0)),
                  vec(CONV_W), vec(CONV_W), vec(CONV_W)],
        out_specs=(out, out),
        scratch_shapes=[pltpu.VMEM((ts + 2 * HALO, POOL_W), F32),
                        pltpu.VMEM((ts + 2 * HALO, CONV_W), F32)],
        compiler_params=_params(2),
        name="pool_conv",
    )(xp, xp, xp, xc, xc, xc, pool_w, pool_scale, conv_dw, conv_b, ln_g, ln_b)


def _merge_body(x_ref, g_ref, o0, o1, o2, l0, l1, l2, yb_ref, yc_ref,
                wg_ref, wa_ref, wb_ref, wc_ref, wo_ref, out_ref, xn_ref, at_ref):
    xf = x_ref[0]
    xn_ref[...] = _rms(xf, g_ref[0]).astype(BF16)
    for h in range(HEADS_PER_GROUP):
        cols = slice(h * HEAD_DIM, (h + 1) * HEAD_DIM)
        ls = [l[0, :, h:h + 1] for l in (l0, l1, l2)]
        mx = jnp.maximum(jnp.maximum(ls[0], ls[1]), ls[2])
        es = [jnp.exp(t - mx) for t in ls]
        den = es[0] + es[1] + es[2]
        acc = sum((e / den) * o[0, :, cols].astype(F32) for e, o in zip(es, (o0, o1, o2)))
        at_ref[:, cols] = acc.astype(BF16)
    d = D_MODEL
    merged = None
    for bi, (br, w) in enumerate(((at_ref[...], wa_ref), (yb_ref[0], wb_ref), (yc_ref[0], wc_ref))):
        gate = _sigmoid(jnp.dot(xn_ref[...], wg_ref[0, :, bi * d:(bi + 1) * d], preferred_element_type=F32))
        y = gate * jnp.dot(br, w[0], preferred_element_type=F32)
        merged = y if merged is None else merged + y
    out_ref[0] = xf + jnp.dot(merged.astype(BF16), wo_ref[0], preferred_element_type=F32)


def _merge(x, norm, os_, lses, yb, yc, w_in, w_a, w_b, w_c, w_out, layer, *, tm=512):
    b, s, d = x.shape
    tm = min(tm, s)
    row = lambda w: pl.BlockSpec((1, tm, w), lambda i, j: (i, j, 0))
    wbr = _resident((1, ATTN_W, d), (layer, 0, 0))
    return pl.pallas_call(
        _merge_body,
        out_shape=jax.ShapeDtypeStruct(x.shape, F32),
        grid=(b, s // tm),
        in_specs=[row(d), _resident((1, 1, d), (layer, 0, 0)),
                  row(ATTN_W), row(ATTN_W), row(ATTN_W),
                  row(HEAD_DIM), row(HEAD_DIM), row(HEAD_DIM),
                  row(POOL_W), row(CONV_W),
                  pl.BlockSpec((1, d, 3 * d), lambda i, j: (layer, 0, PROJ_W // (3 * d)),
                               pipeline_mode=pl.Buffered(1)),
                  wbr, wbr, wbr, _resident((1, d, d), (layer, 0, 0))],
        out_specs=row(d),
        scratch_shapes=[pltpu.VMEM((tm, d), BF16), pltpu.VMEM((tm, ATTN_W), BF16)],
        compiler_params=_params(2),
        name="merge_out",
    )(x, norm, *os_, *lses, yb, yc, w_in, w_a, w_b, w_c, w_out)


def _memkv_body(m_ref, g_ref, w_ref, k_ref, v_ref):
    mn = _rms(m_ref[0], g_ref[0]).astype(BF16)
    kv = jnp.dot(mn, w_ref[0], preferred_element_type=F32)
    k_ref[0] = kv[:, :X_W].astype(BF16)
    v_ref[0] = kv[:, X_W:].astype(BF16)


def _memkv(mem, norm, wkv, layer):
    b, m, d = mem.shape
    out = pl.BlockSpec((1, m, X_W), lambda i: (i, 0, 0))
    return pl.pallas_call(
        _memkv_body,
        out_shape=(jax.ShapeDtypeStruct((b, m, X_W), BF16),) * 2,
        grid=(b,),
        in_specs=[pl.BlockSpec((1, m, d), lambda i: (i, 0, 0)),
                  _resident((1, 1, d), (layer, 0, 0)),
                  _resident((1, d, 2 * X_W), (layer, 0, 0))],
        out_specs=(out, out),
        compiler_params=_params(1),
        name="mem_kv",
    )(mem, norm, wkv)


def _xattn_body(x_ref, g_ref, wq_ref, k_ref, v_ref, wo_ref, out_ref, q_scr, o_scr):
    xf = x_ref[0]
    xn = _rms(xf, g_ref[0]).astype(BF16)
    q_scr[...] = jnp.dot(xn, wq_ref[0], preferred_element_type=F32).astype(BF16)
    scale = HEAD_DIM ** -0.5
    for h in range(X_HEADS):
        cols = slice(h * HEAD_DIM, (h + 1) * HEAD_DIM)
        s = lax.dot_general(q_scr[:, cols], k_ref[0, :, cols], (((1,), (1,)), ((), ())),
                            preferred_element_type=F32) * scale
        p = jnp.exp(s - jnp.max(s, axis=-1, keepdims=True))
        l = jnp.sum(p, axis=-1, keepdims=True)
        o = jnp.dot(p.astype(BF16), v_ref[0, :, cols], preferred_element_type=F32) / l
        o_scr[:, cols] = o.astype(BF16)
    out_ref[0] = xf + jnp.dot(o_scr[...], wo_ref[0], preferred_element_type=F32)


def _xattn(x, norm, wq, kmem, vmem, wo, layer, *, tm=512):
    b, s, d = x.shape
    m = kmem.shape[1]
    tm = min(tm, s)
    row = pl.BlockSpec((1, tm, d), lambda i, j: (i, j, 0))
    mem = pl.BlockSpec((1, m, X_W), lambda i, j: (i, 0, 0))
    return pl.pallas_call(
        _xattn_body,
        out_shape=jax.ShapeDtypeStruct(x.shape, F32),
        grid=(b, s // tm),
        in_specs=[row, _resident((1, 1, d), (layer, 0, 0)), _resident((1, d, X_W), (layer, 0, 0)),
                  mem, mem, _resident((1, X_W, d), (layer, 0, 0))],
        out_specs=row,
        scratch_shapes=[pltpu.VMEM((tm, X_W), BF16), pltpu.VMEM((tm, X_W), BF16)],
        compiler_params=_params(2),
        name="mem_xattn",
    )(x, norm, wq, kmem, vmem, wo)


def _trunk(x, mem, w, biases, depth):
    for l in range(depth):
        x = _ffn(x, w["ffn1_norm"], w["ffn1_w_gu"], w["ffn1_w_down"], w["final_norm"], l, final=False)
        q, k, v, xp, xc = _proj(x, w["mix_norm"], w["w_in"], l)
        outs = [_attn_group(q, k, v, biases[g], g, r) for g, (_, r) in enumerate(DIL_PAIRS)]
        yb, yc = _mixers(xp, xc, w["pool_w"], w["pool_scale"], w["conv_dw"], w["conv_b"],
                         w["conv_ln_g"], w["conv_ln_b"], l)
        x = _merge(x, w["mix_norm"], [o for o, _ in outs], [s for _, s in outs], yb, yc,
                   w["w_in"], w["w_br_attn"], w["w_br_pool"], w["w_br_conv"], w["w_out"], l)
        kmem, vmem = _memkv(mem, w["mem_norm"], w["xattn_wkv"], l)
        x = _xattn(x, w["xattn_norm"], w["xattn_wq"], kmem, vmem, w["xattn_wo"], l)
        x = _ffn(x, w["ffn2_norm"], w["ffn2_w_gu"], w["ffn2_w_down"], w["final_norm"], l,
                 final=(l == depth - 1))
    return x


def kernel(x_prompt, x_sample, mem_prompt, mem_sample, rel_bias, ffn1_norm, ffn1_w_gu, ffn1_w_down, mix_norm, w_in, pool_w, pool_scale, conv_dw, conv_b, conv_ln_g, conv_ln_b, w_br_attn, w_br_pool, w_br_conv, w_out, xattn_norm, mem_norm, xattn_wq, xattn_wkv, xattn_wo, ffn2_norm, ffn2_w_gu, ffn2_w_down, final_norm):
    depth = w_in.shape[0]
    mat = lambda t: t.astype(BF16)
    vec = lambda t: t.reshape(depth, 1, t.shape[-1])
    w = dict(
        ffn1_norm=vec(ffn1_norm), ffn1_w_gu=mat(ffn1_w_gu), ffn1_w_down=mat(ffn1_w_down),
        mix_norm=vec(mix_norm), w_in=mat(w_in), pool_w=mat(pool_w), pool_scale=vec(pool_scale),
        conv_dw=conv_dw, conv_b=vec(conv_b), conv_ln_g=vec(conv_ln_g), conv_ln_b=vec(conv_ln_b),
        w_br_attn=mat(w_br_attn), w_br_pool=mat(w_br_pool), w_br_conv=mat(w_br_conv), w_out=mat(w_out),
        xattn_norm=vec(xattn_norm), mem_norm=vec(mem_norm), xattn_wq=mat(xattn_wq),
        xattn_wkv=mat(xattn_wkv), xattn_wo=mat(xattn_wo),
        ffn2_norm=vec(ffn2_norm), ffn2_w_gu=mat(ffn2_w_gu), ffn2_w_down=mat(ffn2_w_down),
        final_norm=final_norm.reshape(1, -1),
    )
    biases = [_band_bias(rel_bias, g, r) for g, (_, r) in enumerate(DIL_PAIRS)]
    y_prompt = _trunk(x_prompt, mem_prompt, w, biases, depth)
    y_sample = _trunk(x_sample, mem_sample, w, biases, depth)
    return (y_prompt, y_sample)
```

```python
import functools
import math

import jax
import jax.numpy as jnp
from jax import lax
from jax.experimental import pallas as pl
from jax.experimental.pallas import tpu as pltpu

F32 = jnp.float32
BF16 = jnp.bfloat16

D_MODEL = 1024
D_FF = 2816
DIL_PAIRS = ((128, 1), (512, 4), (2048, 16))
HEADS_PER_GROUP = 4
N_HEADS_A = len(DIL_PAIRS) * HEADS_PER_GROUP
HEAD_DIM = 128
QKV_W = N_HEADS_A * HEAD_DIM
ATTN_W = HEADS_PER_GROUP * HEAD_DIM
BAND_HALF = 64
REL_BUCKETS = 32
REL_MAX_EXACT = 8
REL_MAX_DIST = 1024
POOL_SIZES = (2, 4, 8, 16)
POOL_GROUP = 128
POOL_W = len(POOL_SIZES) * POOL_GROUP
CONV_W = 512
CONV_K = 31
CONV_GROUPS = CONV_W // HEAD_DIM
X_HEADS = 4
X_W = X_HEADS * HEAD_DIM
PROJ_W = 3 * QKV_W + POOL_W + 2 * CONV_W
EPS = 1e-6
NEG_INF = -1e30

HALO = 16
Q_SUB = 128
VMEM_LIMIT = 56 * 1024 * 1024


def _params(n_axes):
    return pltpu.CompilerParams(dimension_semantics=("arbitrary",) * n_axes,
                                vmem_limit_bytes=VMEM_LIMIT)


def _resident(shape, index):
    return pl.BlockSpec(shape, lambda *_: index, pipeline_mode=pl.Buffered(1))


def _rms(xf, g):
    ms = jnp.mean(xf * xf, axis=-1, keepdims=True)
    return xf * lax.rsqrt(ms + EPS) * g


def _sigmoid(x):
    return 1.0 / (1.0 + jnp.exp(-x))


def _ffn_body(x_ref, g_ref, wgu_ref, wd_ref, fg_ref, o_ref, xn_ref, act_ref, *, ck, final):
    xf = x_ref[0]
    xn_ref[...] = _rms(xf, g_ref[0]).astype(BF16)
    for c in range(D_FF // ck):
        xn = xn_ref[...]
        a = jnp.dot(xn, wgu_ref[0, :, c * ck:(c + 1) * ck], preferred_element_type=F32)
        u = jnp.dot(xn, wgu_ref[0, :, D_FF + c * ck:D_FF + (c + 1) * ck], preferred_element_type=F32)
        act_ref[:, c * ck:(c + 1) * ck] = (a * _sigmoid(a) * u).astype(BF16)
    y = xf + 0.5 * jnp.dot(act_ref[...], wd_ref[0], preferred_element_type=F32)
    if final:
        y = _rms(y, fg_ref[...])
    o_ref[0] = y


def _ffn(x, norm, w_gu, w_down, final_g, layer, *, final, tm=512, ck=256):
    b, s, d = x.shape
    tm = min(tm, s)
    body = functools.partial(_ffn_body, ck=ck, final=final)
    return pl.pallas_call(
        body,
        out_shape=jax.ShapeDtypeStruct(x.shape, F32),
        grid=(b, s // tm),
        in_specs=[
            pl.BlockSpec((1, tm, d), lambda i, j: (i, j, 0)),
            _resident((1, 1, d), (layer, 0, 0)),
            _resident((1, d, 2 * D_FF), (layer, 0, 0)),
            _resident((1, D_FF, d), (layer, 0, 0)),
            _resident((1, d), (0, 0)),
        ],
        out_specs=pl.BlockSpec((1, tm, d), lambda i, j: (i, j, 0)),
        scratch_shapes=[pltpu.VMEM((tm, d), BF16), pltpu.VMEM((tm, D_FF), BF16)],
        compiler_params=_params(2),
        name="ffn",
    )(x, norm, w_gu, w_down, final_g)


def _proj_body(x_ref, g_ref, w_ref, qkv0_ref, qkv1_ref, qkv2_ref, xp_ref, xc_ref, xn_ref, cls_ref):
    tm = xn_ref.shape[0]
    xn_ref[...] = _rms(x_ref[0], g_ref[0]).astype(BF16)

    def chunk(base):
        return jnp.dot(xn_ref[...], w_ref[0, :, base:base + ATTN_W], preferred_element_type=F32)

    for part in range(3):
        for g, (ref, (_, r)) in enumerate(zip((qkv0_ref, qkv1_ref, qkv2_ref), DIL_PAIRS)):
            y = chunk(part * QKV_W + g * ATTN_W)
            if r == 1:
                ref[0, 0, :, part * ATTN_W:(part + 1) * ATTN_W] = y.astype(BF16)
                continue
            for h in range(HEADS_PER_GROUP):
                cls_ref[h] = y[:, h * HEAD_DIM:(h + 1) * HEAD_DIM]
            for c in range(r):
                for h in range(HEADS_PER_GROUP):
                    col = part * ATTN_W + h * HEAD_DIM
                    ref[0, c, :, col:col + HEAD_DIM] = (
                        cls_ref[h, pl.ds(c, tm // r, stride=r), :].astype(BF16))
    xp_ref[0] = chunk(3 * QKV_W)
    for c in range(2):
        xc_ref[0, :, c * CONV_W:(c + 1) * CONV_W] = chunk(3 * QKV_W + POOL_W + c * CONV_W)


def _proj(x, norm, w_in, layer, *, tm=512):
    b, s, d = x.shape
    tm = min(tm, s)
    row = lambda w: pl.BlockSpec((1, tm, w), lambda i, j: (i, j, 0))
    rs = [r for _, r in DIL_PAIRS]
    qkv_shape = lambda r: jax.ShapeDtypeStruct((b, r, s // r, 3 * ATTN_W), BF16)
    qkv_spec = lambda r: pl.BlockSpec((1, r, tm // r, 3 * ATTN_W), lambda i, j: (i, 0, j, 0))
    return pl.pallas_call(
        _proj_body,
        out_shape=(*[qkv_shape(r) for r in rs],
                   jax.ShapeDtypeStruct((b, s, POOL_W), F32),
                   jax.ShapeDtypeStruct((b, s, 2 * CONV_W), F32)),
        grid=(b, s // tm),
        in_specs=[row(d), _resident((1, 1, d), (layer, 0, 0)),
                  _resident((1, d, PROJ_W), (layer, 0, 0))],
        out_specs=(*[qkv_spec(r) for r in rs], row(POOL_W), row(2 * CONV_W)),
        scratch_shapes=[pltpu.VMEM((tm, d), BF16),
                        pltpu.VMEM((HEADS_PER_GROUP, tm, HEAD_DIM), F32)],
        compiler_params=_params(2),
        name="in_proj",
    )(x, norm, w_in)


def _attn_body(q_ref, kp_ref, kc_ref, kn_ref, vp_ref, vc_ref, vn_ref, bias_ref,
               o_ref, lse_ref, kbuf, vbuf, *, tl, seq_l):
    h64 = BAND_HALF
    kbuf[0:h64] = kp_ref[...]
    kbuf[h64:h64 + tl] = kc_ref[...]
    kbuf[h64 + tl:h64 + tl + h64] = kn_ref[...]
    vbuf[0:h64] = vp_ref[...]
    vbuf[h64:h64 + tl] = vc_ref[...]
    vbuf[h64 + tl:h64 + tl + h64] = vn_ref[...]
    row0 = pl.program_id(2) * tl
    scale = HEAD_DIM ** -0.5
    lane = lax.broadcasted_iota(jnp.int32, (Q_SUB, HEAD_DIM), 1)
    for j in range(tl // Q_SUB):
        rows = slice(j * Q_SUB, (j + 1) * Q_SUB)
        kpos = row0 + (j * Q_SUB - h64) + lax.broadcasted_iota(jnp.int32, (1, 2 * Q_SUB), 1)
        kmask = jnp.where((kpos >= 0) & (kpos < seq_l), 0.0, NEG_INF).astype(F32)
        lse_blk = jnp.zeros((Q_SUB, HEAD_DIM), F32)
        for h in range(HEADS_PER_GROUP):
            cols = slice(h * HEAD_DIM, (h + 1) * HEAD_DIM)
            q = q_ref[rows, cols]
            k = kbuf[j * Q_SUB:(j + 2) * Q_SUB, cols]
            v = vbuf[j * Q_SUB:(j + 2) * Q_SUB, cols]
            s = lax.dot_general(q, k, (((1,), (1,)), ((), ())), preferred_element_type=F32)
            s = s * scale + bias_ref[h] + kmask
            mx = jnp.max(s, axis=-1, keepdims=True)
            p = jnp.exp(s - mx)
            l = jnp.sum(p, axis=-1, keepdims=True)
            o = jnp.dot(p.astype(BF16), v, preferred_element_type=F32) / l
            o_ref[rows, cols] = o.astype(o_ref.dtype)
            lse_blk = jnp.where(lane == h, mx + jnp.log(l), lse_blk)
        lse_ref[rows, :] = lse_blk


def _attn_group(qkv, bias):
    b, r, sl, _ = qkv.shape
    tl = min(512, sl)
    nh = sl // BAND_HALF
    per = tl // BAND_HALF
    cur = lambda part: pl.BlockSpec((None, None, tl, ATTN_W), lambda i, c, m: (i, c, m, part))
    prev = lambda part: pl.BlockSpec((None, None, BAND_HALF, ATTN_W),
                                     lambda i, c, m: (i, c, jnp.maximum(m * per - 1, 0), part))
    nxt = lambda part: pl.BlockSpec((None, None, BAND_HALF, ATTN_W),
                                    lambda i, c, m: (i, c, jnp.minimum((m + 1) * per, nh - 1), part))
    return pl.pallas_call(
        functools.partial(_attn_body, tl=tl, seq_l=sl),
        out_shape=(jax.ShapeDtypeStruct((b, r, sl, ATTN_W), BF16),
                   jax.ShapeDtypeStruct((b, r, sl, HEAD_DIM), F32)),
        grid=(b, r, sl // tl),
        in_specs=[cur(0), prev(1), cur(1), nxt(1), prev(2), cur(2), nxt(2),
                  _resident((HEADS_PER_GROUP, Q_SUB, 2 * Q_SUB), (0, 0, 0))],
        out_specs=(pl.BlockSpec((None, None, tl, ATTN_W), lambda i, c, m: (i, c, m, 0)),
                   pl.BlockSpec((None, None, tl, HEAD_DIM), lambda i, c, m: (i, c, m, 0))),
        scratch_shapes=[pltpu.VMEM((tl + 2 * BAND_HALF, ATTN_W), BF16),
                        pltpu.VMEM((tl + 2 * BAND_HALF, ATTN_W), BF16)],
        compiler_params=_params(3),
        name=f"dilated_attn_r{r}",
    )(qkv, qkv, qkv, qkv, qkv, qkv, qkv, bias)


def _t5_bucket(rel):
    half = REL_BUCKETS // 2
    n = jnp.abs(rel)
    nf = jnp.maximum(n, 1).astype(F32)
    large = REL_MAX_EXACT + (jnp.log(nf / REL_MAX_EXACT) / math.log(REL_MAX_DIST / REL_MAX_EXACT)
                             * (half - REL_MAX_EXACT)).astype(jnp.int32)
    large = jnp.minimum(large, half - 1)
    return jnp.where(rel > 0, half, 0) + jnp.where(n < REL_MAX_EXACT, n, large)


def _band_bias(rel_bias, g, r):
    delta = jnp.arange(2 * Q_SUB)[None, :] - BAND_HALF - jnp.arange(Q_SUB)[:, None]
    bucket = _t5_bucket(r * delta)
    table = rel_bias[:, g * HEADS_PER_GROUP:(g + 1) * HEADS_PER_GROUP].astype(F32)
    onehot = bucket[None, :, :, None] == jnp.arange(REL_BUCKETS)
    tbl = jnp.sum(jnp.where(onehot, table.T[:, None, None, :], 0.0), axis=-1)
    return jnp.where((jnp.abs(delta) <= BAND_HALF)[None], tbl, NEG_INF)


def _mix_body(xp_p, xp_c, xp_n, xc_p, xc_c, xc_n, pw_ref, ps_ref, dw_ref, cb_ref, lg_ref, lb_ref,
              yb_ref, yc_ref, pbuf, ubuf, sbuf, *, ts, seq):
    i = pl.program_id(1)
    has_prev = (i > 0).astype(F32)
    has_next = (i < pl.num_programs(1) - 1).astype(F32)
    half = ts // 2

    def fill(buf, g, prev, cur, nxt):
        buf[g, 0:HALO] = prev * has_prev
        buf[g, HALO:HALO + ts] = cur
        buf[g, HALO + ts:HALO + ts + HALO] = nxt * has_next

    def shifted(buf, g, off, par):
        return buf[g, pl.ds(HALO + off + par, half, stride=2), :]

    pos = i * ts + lax.broadcasted_iota(jnp.int32, (ts, 1), 0)
    for g, kw in enumerate(POOL_SIZES):
        cols = slice(g * POOL_GROUP, (g + 1) * POOL_GROUP)
        fill(pbuf, g, xp_p[0, :, cols], xp_c[0, :, cols], xp_n[0, :, cols])
        for par in range(2):
            acc = shifted(pbuf, g, -(kw // 2), par)
            for dlt in range(-(kw // 2) + 1, kw // 2):
                acc = acc + shifted(pbuf, g, dlt, par)
            sbuf[g, pl.ds(par, half, stride=2), :] = acc
        cnt = (jnp.minimum(pos + kw // 2, seq) - jnp.maximum(pos - kw // 2, 0)).astype(F32)
        pooled = (sbuf[g] / cnt - xp_c[0, :, cols]).astype(BF16)
        mixed = jnp.dot(pooled, pw_ref[0, g], preferred_element_type=F32)
        yb_ref[0, :, cols] = (mixed * ps_ref[0, :, cols]).astype(yb_ref.dtype)

    def glu(ref, g):
        return ref[0, :, g * HEAD_DIM:(g + 1) * HEAD_DIM] * _sigmoid(
            ref[0, :, CONV_W + g * HEAD_DIM:CONV_W + (g + 1) * HEAD_DIM])
    for g in range(CONV_GROUPS):
        cols = slice(g * HEAD_DIM, (g + 1) * HEAD_DIM)
        fill(ubuf, g, glu(xc_p, g), glu(xc_c, g), glu(xc_n, g))
        for par in range(2):
            acc = jnp.zeros((half, HEAD_DIM), F32) + cb_ref[0, :, cols]
            for kk in range(CONV_K):
                acc = acc + dw_ref[0, kk:kk + 1, cols] * shifted(ubuf, g, kk - CONV_K // 2, par)
            sbuf[g, pl.ds(par, half, stride=2), :] = acc
    tot = sum(jnp.sum(sbuf[g], axis=-1, keepdims=True) for g in range(CONV_GROUPS))
    mu = tot / CONV_W
    sq = sum(jnp.sum((sbuf[g] - mu) ** 2, axis=-1, keepdims=True) for g in range(CONV_GROUPS))
    inv = lax.rsqrt(sq / CONV_W + EPS)
    for g in range(CONV_GROUPS):
        cols = slice(g * HEAD_DIM, (g + 1) * HEAD_DIM)
        y = (sbuf[g] - mu) * inv * lg_ref[0, :, cols] + lb_ref[0, :, cols]
        yc_ref[0, :, cols] = (y * _sigmoid(y)).astype(yc_ref.dtype)


def _mixers(xp, xc, pool_w, pool_scale, conv_dw, conv_b, ln_g, ln_b, layer, *, ts=256):
    b, s, _ = xp.shape
    ts = min(ts, s)
    per = ts // HALO
    nh = s // HALO

    def halo_specs(w):
        return (pl.BlockSpec((1, HALO, w), lambda i, j: (i, jnp.maximum(j * per - 1, 0), 0)),
                pl.BlockSpec((1, ts, w), lambda i, j: (i, j, 0)),
                pl.BlockSpec((1, HALO, w), lambda i, j: (i, jnp.minimum((j + 1) * per, nh - 1), 0)))

    vec = lambda w: _resident((1, 1, w), (layer, 0, 0))
    out = pl.BlockSpec((1, ts, POOL_W), lambda i, j: (i, j, 0))
    return pl.pallas_call(
        functools.partial(_mix_body, ts=ts, seq=s),
        out_shape=(jax.ShapeDtypeStruct((b, s, POOL_W), BF16),
                   jax.ShapeDtypeStruct((b, s, CONV_W), BF16)),
        grid=(b, s // ts),
        in_specs=[*halo_specs(POOL_W), *halo_specs(2 * CONV_W),
                  _resident((1, len(POOL_SIZES), POOL_GROUP, POOL_GROUP), (layer, 0, 0, 0)),
                  vec(POOL_W),
                  _resident((1, CONV_K, CONV_W), (layer, 0, 0)),
                  vec(CONV_W), vec(CONV_W), vec(CONV_W)],
        out_specs=(out, out),
        scratch_shapes=[pltpu.VMEM((len(POOL_SIZES), ts + 2 * HALO, POOL_GROUP), F32),
                        pltpu.VMEM((CONV_GROUPS, ts + 2 * HALO, HEAD_DIM), F32),
                        pltpu.VMEM((CONV_GROUPS, ts, HEAD_DIM), F32)],
        compiler_params=_params(2),
        name="pool_conv",
    )(xp, xp, xp, xc, xc, xc, pool_w, pool_scale, conv_dw, conv_b, ln_g, ln_b)


def _merge_body(x_ref, g_ref, o0, o1, o2, l0, l1, l2, yb_ref, yc_ref,
                wg_ref, wa_ref, wb_ref, wc_ref, wo_ref, out_ref, xn_ref, at_ref, obuf, lbuf):
    tm = xn_ref.shape[0]
    xf = x_ref[0]
    xn_ref[...] = _rms(xf, g_ref[0]).astype(BF16)
    for gi, (o, l, (_, r)) in enumerate(zip((o0, o1, o2), (l0, l1, l2), DIL_PAIRS)):
        for c in range(r):
            dst = pl.ds(c, tm // r, stride=r) if r > 1 else pl.ds(0, tm)
            lbuf[gi, dst, :] = l[0, c]
            for h in range(HEADS_PER_GROUP):
                obuf[gi, h, dst, :] = o[0, c, :, h * HEAD_DIM:(h + 1) * HEAD_DIM].astype(F32)
    n_g = len(DIL_PAIRS)
    for h in range(HEADS_PER_GROUP):
        ls = [lbuf[gi, :, h:h + 1] for gi in range(n_g)]
        mx = jnp.maximum(jnp.maximum(ls[0], ls[1]), ls[2])
        es = [jnp.exp(t - mx) for t in ls]
        den = es[0] + es[1] + es[2]
        acc = sum((es[gi] / den) * obuf[gi, h] for gi in range(n_g))
        at_ref[:, h * HEAD_DIM:(h + 1) * HEAD_DIM] = acc.astype(BF16)
    d = D_MODEL
    merged = None
    for bi, (br, w) in enumerate(((at_ref[...], wa_ref), (yb_ref[0], wb_ref), (yc_ref[0], wc_ref))):
        gate = _sigmoid(jnp.dot(xn_ref[...], wg_ref[0, :, bi * d:(bi + 1) * d], preferred_element_type=F32))
        y = gate * jnp.dot(br, w[0], preferred_element_type=F32)
        merged = y if merged is None else merged + y
    out_ref[0] = xf + jnp.dot(merged.astype(BF16), wo_ref[0], preferred_element_type=F32)


def _merge(x, norm, os_, lses, yb, yc, w_in, w_a, w_b, w_c, w_out, layer, *, tm=512):
    b, s, d = x.shape
    tm = min(tm, s)
    row = lambda w: pl.BlockSpec((1, tm, w), lambda i, j: (i, j, 0))
    cls = lambda r, w: pl.BlockSpec((1, r, tm // r, w), lambda i, j: (i, 0, j, 0))
    rs = [r for _, r in DIL_PAIRS]
    wbr = _resident((1, ATTN_W, d), (layer, 0, 0))
    return pl.pallas_call(
        _merge_body,
        out_shape=jax.ShapeDtypeStruct(x.shape, F32),
        grid=(b, s // tm),
        in_specs=[row(d), _resident((1, 1, d), (layer, 0, 0)),
                  *[cls(r, ATTN_W) for r in rs], *[cls(r, HEAD_DIM) for r in rs],
                  row(POOL_W), row(CONV_W),
                  pl.BlockSpec((1, d, 3 * d), lambda i, j: (layer, 0, PROJ_W // (3 * d)),
                               pipeline_mode=pl.Buffered(1)),
                  wbr, wbr, wbr, _resident((1, d, d), (layer, 0, 0))],
        out_specs=row(d),
        scratch_shapes=[pltpu.VMEM((tm, d), BF16), pltpu.VMEM((tm, ATTN_W), BF16),
                        pltpu.VMEM((len(rs), HEADS_PER_GROUP, tm, HEAD_DIM), F32),
                        pltpu.VMEM((len(rs), tm, HEAD_DIM), F32)],
        compiler_params=_params(2),
        name="merge_out",
    )(x, norm, *os_, *lses, yb, yc, w_in, w_a, w_b, w_c, w_out)


def _memkv_body(m_ref, g_ref, w_ref, k_ref, v_ref):
    mn = _rms(m_ref[0], g_ref[0]).astype(BF16)
    kv = jnp.dot(mn, w_ref[0], preferred_element_type=F32)
    k_ref[0] = kv[:, :X_W].astype(BF16)
    v_ref[0] = kv[:, X_W:].astype(BF16)


def _memkv(mem, norm, wkv, layer):
    b, m, d = mem.shape
    out = pl.BlockSpec((1, m, X_W), lambda i: (i, 0, 0))
    return pl.pallas_call(
        _memkv_body,
        out_shape=(jax.ShapeDtypeStruct((b, m, X_W), BF16),) * 2,
        grid=(b,),
        in_specs=[pl.BlockSpec((1, m, d), lambda i: (i, 0, 0)),
                  _resident((1, 1, d), (layer, 0, 0)),
                  _resident((1, d, 2 * X_W), (layer, 0, 0))],
        out_specs=(out, out),
        compiler_params=_params(1),
        name="mem_kv",
    )(mem, norm, wkv)


def _xattn_body(x_ref, g_ref, wq_ref, k_ref, v_ref, wo_ref, out_ref, q_scr, o_scr):
    xf = x_ref[0]
    xn = _rms(xf, g_ref[0]).astype(BF16)
    q_scr[...] = jnp.dot(xn, wq_ref[0], preferred_element_type=F32).astype(BF16)
    scale = HEAD_DIM ** -0.5
    for h in range(X_HEADS):
        cols = slice(h * HEAD_DIM, (h + 1) * HEAD_DIM)
        s = lax.dot_general(q_scr[:, cols], k_ref[0, :, cols], (((1,), (1,)), ((), ())),
                            preferred_element_type=F32) * scale
        p = jnp.exp(s - jnp.max(s, axis=-1, keepdims=True))
        l = jnp.sum(p, axis=-1, keepdims=True)
        o = jnp.dot(p.astype(BF16), v_ref[0, :, cols], preferred_element_type=F32) / l
        o_scr[:, cols] = o.astype(BF16)
    out_ref[0] = xf + jnp.dot(o_scr[...], wo_ref[0], preferred_element_type=F32)


def _xattn(x, norm, wq, kmem, vmem, wo, layer, *, tm=512):
    b, s, d = x.shape
    m = kmem.shape[1]
    tm = min(tm, s)
    row = pl.BlockSpec((1, tm, d), lambda i, j: (i, j, 0))
    mem = pl.BlockSpec((1, m, X_W), lambda i, j: (i, 0, 0))
    return pl.pallas_call(
        _xattn_body,
        out_shape=jax.ShapeDtypeStruct(x.shape, F32),
        grid=(b, s // tm),
        in_specs=[row, _resident((1, 1, d), (layer, 0, 0)), _resident((1, d, X_W), (layer, 0, 0)),
                  mem, mem, _resident((1, X_W, d), (layer, 0, 0))],
        out_specs=row,
        scratch_shapes=[pltpu.VMEM((tm, X_W), BF16), pltpu.VMEM((tm, X_W), BF16)],
        compiler_params=_params(2),
        name="mem_xattn",
    )(x, norm, wq, kmem, vmem, wo)


def _trunk(x, mem, w, biases, depth):
    for l in range(depth):
        x = _ffn(x, w["ffn1_norm"], w["ffn1_w_gu"], w["ffn1_w_down"], w["final_norm"], l, final=False)
        *qkvs, xp, xc = _proj(x, w["mix_norm"], w["w_in"], l)
        outs = [_attn_group(qkv, bias) for qkv, bias in zip(qkvs, biases)]
        yb, yc = _mixers(xp, xc, w["pool_w"], w["pool_scale"], w["conv_dw"], w["conv_b"],
                         w["conv_ln_g"], w["conv_ln_b"], l)
        x = _merge(x, w["mix_norm"], [o for o, _ in outs], [s for _, s in outs], yb, yc,
                   w["w_in"], w["w_br_attn"], w["w_br_pool"], w["w_br_conv"], w["w_out"], l)
        kmem, vmem = _memkv(mem, w["mem_norm"], w["xattn_wkv"], l)
        x = _xattn(x, w["xattn_norm"], w["xattn_wq"], kmem, vmem, w["xattn_wo"], l)
        x = _ffn(x, w["ffn2_norm"], w["ffn2_w_gu"], w["ffn2_w_down"], w["final_norm"], l,
                 final=(l == depth - 1))
    return x


def kernel(x_prompt, x_sample, mem_prompt, mem_sample, rel_bias, ffn1_norm, ffn1_w_gu, ffn1_w_down, mix_norm, w_in, pool_w, pool_scale, conv_dw, conv_b, conv_ln_g, conv_ln_b, w_br_attn, w_br_pool, w_br_conv, w_out, xattn_norm, mem_norm, xattn_wq, xattn_wkv, xattn_wo, ffn2_norm, ffn2_w_gu, ffn2_w_down, final_norm):
    depth = w_in.shape[0]
    mat = lambda t: t.astype(BF16)
    vec = lambda t: t.reshape(depth, 1, t.shape[-1])
    w = dict(
        ffn1_norm=vec(ffn1_norm), ffn1_w_gu=mat(ffn1_w_gu), ffn1_w_down=mat(ffn1_w_down),
        mix_norm=vec(mix_norm), w_in=mat(w_in), pool_w=mat(pool_w), pool_scale=vec(pool_scale),
        conv_dw=conv_dw, conv_b=vec(conv_b), conv_ln_g=vec(conv_ln_g), conv_ln_b=vec(conv_ln_b),
        w_br_attn=mat(w_br_attn), w_br_pool=mat(w_br_pool), w_br_conv=mat(w_br_conv), w_out=mat(w_out),
        xattn_norm=vec(xattn_norm), mem_norm=vec(mem_norm), xattn_wq=mat(xattn_wq),
        xattn_wkv=mat(xattn_wkv), xattn_wo=mat(xattn_wo),
        ffn2_norm=vec(ffn2_norm), ffn2_w_gu=mat(ffn2_w_gu), ffn2_w_down=mat(ffn2_w_down),
        final_norm=final_norm.reshape(1, -1),
    )
    biases = [_band_bias(rel_bias, g, r) for g, (_, r) in enumerate(DIL_PAIRS)]
    y_prompt = _trunk(x_prompt, mem_prompt, w, biases, depth)
    y_sample = _trunk(x_sample, mem_sample, w, biases, depth)
    return (y_prompt, y_sample)
```

```python
import functools
import math

import jax
import jax.numpy as jnp
from jax import lax
from jax.experimental import pallas as pl
from jax.experimental.pallas import tpu as pltpu

F32 = jnp.float32
BF16 = jnp.bfloat16

D_MODEL = 1024
D_FF = 2816
DIL_PAIRS = ((128, 1), (512, 4), (2048, 16))
HEADS_PER_GROUP = 4
N_HEADS_A = len(DIL_PAIRS) * HEADS_PER_GROUP
HEAD_DIM = 128
QKV_W = N_HEADS_A * HEAD_DIM
ATTN_W = HEADS_PER_GROUP * HEAD_DIM
BAND_HALF = 64
REL_BUCKETS = 32
REL_MAX_EXACT = 8
REL_MAX_DIST = 1024
POOL_SIZES = (2, 4, 8, 16)
POOL_GROUP = 128
POOL_W = len(POOL_SIZES) * POOL_GROUP
CONV_W = 512
CONV_K = 31
CONV_GROUPS = CONV_W // HEAD_DIM
X_HEADS = 4
X_W = X_HEADS * HEAD_DIM
PROJ_W = 3 * QKV_W + POOL_W + 2 * CONV_W
EPS = 1e-6
NEG_INF = -1e30
LOG2_E = math.log2(math.e)
LN_2 = math.log(2.0)

HALO = 16
CONV_CHUNK = 256
GATE_CHUNK = 256
Q_SUB = 128
VMEM_LIMIT = 56 * 1024 * 1024


def _params(n_axes):
    return pltpu.CompilerParams(dimension_semantics=("arbitrary",) * n_axes,
                                vmem_limit_bytes=VMEM_LIMIT)


def _resident(shape, index):
    return pl.BlockSpec(shape, lambda *_: index, pipeline_mode=pl.Buffered(1))


def _rms(xf, g):
    ms = jnp.mean(xf * xf, axis=-1, keepdims=True)
    return xf * lax.rsqrt(ms + EPS) * g


def _sigmoid(x):
    return 1.0 / (1.0 + jnp.exp(-x))


def _zero_after(v):
    bits = pltpu.bitcast(v, jnp.uint32)
    return pltpu.bitcast((bits >> 16) >> 16, F32)


def _ffn_body(x_ref, g_ref, wgu_ref, wd_ref, fg_ref, o_ref, xn_ref, act_ref, *, ck, final):
    xf = x_ref[0]
    xn_ref[...] = _rms(xf, g_ref[0]).astype(BF16)
    for c in range(D_FF // ck):
        xn = xn_ref[...]
        a = jnp.dot(xn, wgu_ref[0, :, c * ck:(c + 1) * ck], preferred_element_type=F32)
        u = jnp.dot(xn, wgu_ref[0, :, D_FF + c * ck:D_FF + (c + 1) * ck], preferred_element_type=F32)
        act_ref[:, c * ck:(c + 1) * ck] = (a * _sigmoid(a) * u).astype(BF16)
    y = xf + 0.5 * jnp.dot(act_ref[...], wd_ref[0], preferred_element_type=F32)
    if final:
        y = _rms(y, fg_ref[...])
    o_ref[0] = y


def _ffn(x, norm, w_gu, w_down, final_g, layer, *, final, tm=512, ck=256):
    b, s, d = x.shape
    tm = min(tm, s)
    body = functools.partial(_ffn_body, ck=ck, final=final)
    return pl.pallas_call(
        body,
        out_shape=jax.ShapeDtypeStruct(x.shape, F32),
        grid=(b, s // tm),
        in_specs=[
            pl.BlockSpec((1, tm, d), lambda i, j: (i, j, 0)),
            _resident((1, 1, d), (layer, 0, 0)),
            _resident((1, d, 2 * D_FF), (layer, 0, 0)),
            _resident((1, D_FF, d), (layer, 0, 0)),
            _resident((1, d), (0, 0)),
        ],
        out_specs=pl.BlockSpec((1, tm, d), lambda i, j: (i, j, 0)),
        scratch_shapes=[pltpu.VMEM((tm, d), BF16), pltpu.VMEM((tm, D_FF), BF16)],
        compiler_params=_params(2),
        name="ffn",
    )(x, norm, w_gu, w_down, final_g)


def _proj_body(x_ref, g_ref, w_ref, qkv0_ref, qkv1_ref, qkv2_ref, xp_ref, xc_ref, xn_ref, cls_ref):
    tm = xn_ref.shape[0]
    xn_ref[...] = _rms(x_ref[0], g_ref[0]).astype(BF16)

    def chunk(base):
        return jnp.dot(xn_ref[...], w_ref[0, :, base:base + ATTN_W], preferred_element_type=F32)

    n_split = 0
    for part in range(3):
        for g, (ref, (_, r)) in enumerate(zip((qkv0_ref, qkv1_ref, qkv2_ref), DIL_PAIRS)):
            y = chunk(part * QKV_W + g * ATTN_W)
            if r == 1:
                ref[0, 0, :, part * ATTN_W:(part + 1) * ATTN_W] = y.astype(BF16)
                continue
            buf = n_split % 2
            n_split += 1
            for h in range(HEADS_PER_GROUP):
                cls_ref[buf, h] = y[:, h * HEAD_DIM:(h + 1) * HEAD_DIM]
            for c in range(r):
                for h in range(HEADS_PER_GROUP):
                    col = part * ATTN_W + h * HEAD_DIM
                    ref[0, c, :, col:col + HEAD_DIM] = (
                        cls_ref[buf, h, pl.ds(c, tm // r, stride=r), :].astype(BF16))
    xp_ref[0] = chunk(3 * QKV_W)
    for c in range(2):
        xc_ref[0, :, c * CONV_W:(c + 1) * CONV_W] = chunk(3 * QKV_W + POOL_W + c * CONV_W)


def _proj(x, norm, w_in, layer, *, tm=512):
    b, s, d = x.shape
    tm = min(tm, s)
    row = lambda w: pl.BlockSpec((1, tm, w), lambda i, j: (i, j, 0))
    rs = [r for _, r in DIL_PAIRS]
    qkv_shape = lambda r: jax.ShapeDtypeStruct((b, r, s // r, 3 * ATTN_W), BF16)
    qkv_spec = lambda r: pl.BlockSpec((1, r, tm // r, 3 * ATTN_W), lambda i, j: (i, 0, j, 0))
    return pl.pallas_call(
        _proj_body,
        out_shape=(*[qkv_shape(r) for r in rs],
                   jax.ShapeDtypeStruct((b, s, POOL_W), F32),
                   jax.ShapeDtypeStruct((b, s, 2 * CONV_W), F32)),
        grid=(b, s // tm),
        in_specs=[row(d), _resident((1, 1, d), (layer, 0, 0)),
                  _resident((1, d, PROJ_W), (layer, 0, 0))],
        out_specs=(*[qkv_spec(r) for r in rs], row(POOL_W), row(2 * CONV_W)),
        scratch_shapes=[pltpu.VMEM((tm, d), BF16),
                        pltpu.VMEM((2, HEADS_PER_GROUP, tm, HEAD_DIM), F32)],
        compiler_params=_params(2),
        name="in_proj",
    )(x, norm, w_in)


def _attn_body(q_ref, kp_ref, kc_ref, kn_ref, vp_ref, vc_ref, vn_ref, bias_ref,
               o_ref, lse_ref, kbuf, vbuf, *, tl, seq_l):
    h64 = BAND_HALF
    kbuf[0:h64] = kp_ref[...]
    kbuf[h64:h64 + tl] = kc_ref[...]
    kbuf[h64 + tl:h64 + tl + h64] = kn_ref[...]
    vbuf[0:h64] = vp_ref[...]
    vbuf[h64:h64 + tl] = vc_ref[...]
    vbuf[h64 + tl:h64 + tl + h64] = vn_ref[...]
    row0 = pl.program_id(2) * tl
    scale2 = HEAD_DIM ** -0.5 * LOG2_E
    lane = lax.broadcasted_iota(jnp.int32, (Q_SUB, HEAD_DIM), 1)
    n_sub = tl // Q_SUB
    for j in range(n_sub):
        rows = slice(j * Q_SUB, (j + 1) * Q_SUB)
        kmask = None
        if j == 0 or j == n_sub - 1:
            kpos = row0 + (j * Q_SUB - h64) + lax.broadcasted_iota(jnp.int32, (1, 2 * Q_SUB), 1)
            kmask = jnp.where((kpos >= 0) & (kpos < seq_l), 0.0, NEG_INF).astype(F32)
        lse_blk = jnp.zeros((Q_SUB, HEAD_DIM), F32)
        for h in range(HEADS_PER_GROUP):
            cols = slice(h * HEAD_DIM, (h + 1) * HEAD_DIM)
            q = q_ref[rows, cols]
            k = kbuf[j * Q_SUB:(j + 2) * Q_SUB, cols]
            v = vbuf[j * Q_SUB:(j + 2) * Q_SUB, cols]
            s = lax.dot_general(q, k, (((1,), (1,)), ((), ())), preferred_element_type=F32)
            s = s * scale2 + bias_ref[h]
            if kmask is not None:
                s = s + kmask
            mx = jnp.max(s, axis=-1, keepdims=True)
            p = jnp.exp2(s - mx)
            l = jnp.sum(p, axis=-1, keepdims=True)
            o = jnp.dot(p.astype(BF16), v, preferred_element_type=F32) / l
            o_ref[rows, cols] = o.astype(o_ref.dtype)
            lse_blk = jnp.where(lane == h, (mx + jnp.log2(l)) * LN_2, lse_blk)
        lse_ref[rows, :] = lse_blk


def _attn_group(qkv, bias):
    b, r, sl, _ = qkv.shape
    tl = min(512, sl)
    nh = sl // BAND_HALF
    per = tl // BAND_HALF
    cur = lambda part: pl.BlockSpec((None, None, tl, ATTN_W), lambda i, c, m: (i, c, m, part))
    prev = lambda part: pl.BlockSpec((None, None, BAND_HALF, ATTN_W),
                                     lambda i, c, m: (i, c, jnp.maximum(m * per - 1, 0), part))
    nxt = lambda part: pl.BlockSpec((None, None, BAND_HALF, ATTN_W),
                                    lambda i, c, m: (i, c, jnp.minimum((m + 1) * per, nh - 1), part))
    return pl.pallas_call(
        functools.partial(_attn_body, tl=tl, seq_l=sl),
        out_shape=(jax.ShapeDtypeStruct((b, r, sl, ATTN_W), BF16),
                   jax.ShapeDtypeStruct((b, r, sl, HEAD_DIM), F32)),
        grid=(b, r, sl // tl),
        in_specs=[cur(0), prev(1), cur(1), nxt(1), prev(2), cur(2), nxt(2),
                  _resident((HEADS_PER_GROUP, Q_SUB, 2 * Q_SUB), (0, 0, 0))],
        out_specs=(pl.BlockSpec((None, None, tl, ATTN_W), lambda i, c, m: (i, c, m, 0)),
                   pl.BlockSpec((None, None, tl, HEAD_DIM), lambda i, c, m: (i, c, m, 0))),
        scratch_shapes=[pltpu.VMEM((tl + 2 * BAND_HALF, ATTN_W), BF16),
                        pltpu.VMEM((tl + 2 * BAND_HALF, ATTN_W), BF16)],
        compiler_params=_params(3),
        name=f"dilated_attn_r{r}",
    )(qkv, qkv, qkv, qkv, qkv, qkv, qkv, bias)


def _t5_bucket(rel):
    half = REL_BUCKETS // 2
    n = jnp.abs(rel)
    nf = jnp.maximum(n, 1).astype(F32)
    large = REL_MAX_EXACT + (jnp.log(nf / REL_MAX_EXACT) / math.log(REL_MAX_DIST / REL_MAX_EXACT)
                             * (half - REL_MAX_EXACT)).astype(jnp.int32)
    large = jnp.minimum(large, half - 1)
    return jnp.where(rel > 0, half, 0) + jnp.where(n < REL_MAX_EXACT, n, large)


def _band_bias(rel_bias, g, r):
    delta = jnp.arange(2 * Q_SUB)[None, :] - BAND_HALF - jnp.arange(Q_SUB)[:, None]
    bucket = _t5_bucket(r * delta)
    table = rel_bias[:, g * HEADS_PER_GROUP:(g + 1) * HEADS_PER_GROUP].astype(F32)
    onehot = bucket[None, :, :, None] == jnp.arange(REL_BUCKETS)
    tbl = jnp.sum(jnp.where(onehot, table.T[:, None, None, :], 0.0), axis=-1)
    return jnp.where((jnp.abs(delta) <= BAND_HALF)[None], tbl * LOG2_E, NEG_INF)


def _pool_conv_steps(xp_p, xp_c, xp_n, xc_p, xc_c, xc_n, pw_ref, ps_ref, dw_ref, cb_ref, lg_ref, lb_ref,
                     yb_ref, yc_ref, pbuf, ubuf, sbuf, *, ts, seq, pace):
    i = pl.program_id(1)
    has_prev = (i > 0).astype(F32)
    has_next = (i < pl.num_programs(1) - 1).astype(F32)
    chunk = min(CONV_CHUNK, ts)
    half = chunk // 2
    steps = []

    def fill(buf, g, prev, cur, nxt):
        buf[g, 0:HALO] = prev * has_prev
        buf[g, HALO:HALO + ts] = cur
        buf[g, HALO + ts:HALO + ts + HALO] = nxt * has_next

    def shifted(buf, g, row0, off, par):
        return buf[g, pl.ds(HALO + row0 + off + par, half, stride=2), :]

    def pool_group(g, kw):
        cols = slice(g * POOL_GROUP, (g + 1) * POOL_GROUP)
        fill(pbuf, g, xp_p[0, :, cols], xp_c[0, :, cols], xp_n[0, :, cols])
        for row0 in range(0, ts, chunk):
            for par in range(2):
                acc = shifted(pbuf, g, row0, -(kw // 2), par)
                for dlt in range(-(kw // 2) + 1, kw // 2):
                    acc = acc + shifted(pbuf, g, row0, dlt, par)
                sbuf[g, pl.ds(row0 + par, half, stride=2), :] = acc
        pos = i * ts + lax.broadcasted_iota(jnp.int32, (ts, 1), 0)
        cnt = (jnp.minimum(pos + kw // 2, seq) - jnp.maximum(pos - kw // 2, 0)).astype(F32)
        pooled = (sbuf[g] / cnt - xp_c[0, :, cols]).astype(BF16)
        mixed = jnp.dot(pooled, pw_ref[0, g], preferred_element_type=F32)
        yb_ref[:, cols] = (mixed * ps_ref[0, :, cols]).astype(yb_ref.dtype)

    for g, kw in enumerate(POOL_SIZES):
        steps.append(functools.partial(pool_group, g, kw))

    def glu(ref, g):
        return ref[0, :, g * HEAD_DIM:(g + 1) * HEAD_DIM] * _sigmoid(
            ref[0, :, CONV_W + g * HEAD_DIM:CONV_W + (g + 1) * HEAD_DIM])

    def conv_fill(g):
        fill(ubuf, g, glu(xc_p, g), glu(xc_c, g), glu(xc_n, g))

    def conv_pass(g, row0, par):
        cols = slice(g * HEAD_DIM, (g + 1) * HEAD_DIM)
        bias = cb_ref[0, :, cols]
        after = pace()
        if after is not None:
            bias = bias + _zero_after(after)
        acc = jnp.zeros((half, HEAD_DIM), F32) + bias
        for kk in range(CONV_K):
            acc = acc + dw_ref[0, kk:kk + 1, cols] * shifted(ubuf, g, row0, kk - CONV_K // 2, par)
        sbuf[g, pl.ds(row0 + par, half, stride=2), :] = acc

    def conv_norm():
        tot = sum(jnp.sum(sbuf[g], axis=-1, keepdims=True) for g in range(CONV_GROUPS))
        mu = tot / CONV_W
        sq = sum(jnp.sum((sbuf[g] - mu) ** 2, axis=-1, keepdims=True) for g in range(CONV_GROUPS))
        inv = lax.rsqrt(sq / CONV_W + EPS)
        for g in range(CONV_GROUPS):
            cols = slice(g * HEAD_DIM, (g + 1) * HEAD_DIM)
            y = (sbuf[g] - mu) * inv * lg_ref[0, :, cols] + lb_ref[0, :, cols]
            yc_ref[:, cols] = (y * _sigmoid(y)).astype(yc_ref.dtype)

    for g in range(CONV_GROUPS):
        steps.append(functools.partial(conv_fill, g))
        for row0 in range(0, ts, chunk):
            for par in range(2):
                steps.append(functools.partial(conv_pass, g, row0, par))
    steps.append(conv_norm)
    return steps


def _emit_interleaved(first, second):
    order = sorted([((k + 1) / len(first), 0, k) for k in range(len(first))]
                   + [((k + 0.5) / len(second), 1, k) for k in range(len(second))])
    for _, which, k in order:
        (first, second)[which][k]()


def _merge_body(x_ref, g_ref, o0, o1, o2, l0, l1, l2, xp_p, xp_c, xp_n, xc_p, xc_c, xc_n,
                pw_ref, ps_ref, dw_ref, cb_ref, lg_ref, lb_ref,
                wg_ref, wa_ref, wb_ref, wc_ref, wo_ref, out_ref,
                xn_ref, at_ref, yb_ref, yc_ref, gate_ref, obuf, lbuf, pbuf, ubuf, sbuf, *, seq):
    tm = xn_ref.shape[0]
    d = D_MODEL
    n_g = len(DIL_PAIRS)
    xn_ref[...] = _rms(x_ref[0], g_ref[0]).astype(BF16)

    def attn_head(h):
        for gi, (o, l, (_, r)) in enumerate(zip((o0, o1, o2), (l0, l1, l2), DIL_PAIRS)):
            for c in range(r):
                dst = pl.ds(c, tm // r, stride=r) if r > 1 else pl.ds(0, tm)
                if h == 0:
                    lbuf[gi, dst, :] = l[0, c]
                obuf[gi, h, dst, :] = o[0, c, :, h * HEAD_DIM:(h + 1) * HEAD_DIM].astype(F32)
        ls = [lbuf[gi, :, h:h + 1] for gi in range(n_g)]
        mx = jnp.maximum(jnp.maximum(ls[0], ls[1]), ls[2])
        es = [jnp.exp(t - mx) for t in ls]
        den = es[0] + es[1] + es[2]
        acc = sum((es[gi] / den) * obuf[gi, h] for gi in range(n_g))
        at_ref[:, h * HEAD_DIM:(h + 1) * HEAD_DIM] = acc.astype(BF16)

    last_gate = [None]

    def gate_chunk(bi, c):
        z = jnp.dot(xn_ref[...], wg_ref[0, :, bi * d + c * GATE_CHUNK:bi * d + (c + 1) * GATE_CHUNK],
                    preferred_element_type=F32)
        gate_ref[bi, :, c * GATE_CHUNK:(c + 1) * GATE_CHUNK] = z
        last_gate[0] = z[tm - 1:tm, GATE_CHUNK - HEAD_DIM:GATE_CHUNK]

    vpu_steps = [functools.partial(attn_head, h) for h in range(HEADS_PER_GROUP)]
    vpu_steps += _pool_conv_steps(xp_p, xp_c, xp_n, xc_p, xc_c, xc_n, pw_ref, ps_ref, dw_ref, cb_ref,
                                  lg_ref, lb_ref, yb_ref, yc_ref, pbuf, ubuf, sbuf, ts=tm, seq=seq,
                                  pace=lambda: last_gate[0])
    mxu_steps = [functools.partial(gate_chunk, bi, c) for bi in range(3) for c in range(d // GATE_CHUNK)]
    _emit_interleaved(vpu_steps, mxu_steps)

    merged = None
    for bi, (br, w) in enumerate(((at_ref, wa_ref), (yb_ref, wb_ref), (yc_ref, wc_ref))):
        y = _sigmoid(gate_ref[bi]) * jnp.dot(br[...], w[0], preferred_element_type=F32)
        merged = y if merged is None else merged + y
    out_ref[0] = x_ref[0] + jnp.dot(merged.astype(BF16), wo_ref[0], preferred_element_type=F32)


def _merge(x, norm, os_, lses, xp, xc, w, layer, *, tm=512):
    b, s, d = x.shape
    tm = min(tm, s)
    per = tm // HALO
    nh = s // HALO
    row = lambda wd: pl.BlockSpec((1, tm, wd), lambda i, j: (i, j, 0))
    cls = lambda r, wd: pl.BlockSpec((1, r, tm // r, wd), lambda i, j: (i, 0, j, 0))

    def halo_specs(wd):
        return (pl.BlockSpec((1, HALO, wd), lambda i, j: (i, jnp.maximum(j * per - 1, 0), 0)),
                row(wd),
                pl.BlockSpec((1, HALO, wd), lambda i, j: (i, jnp.minimum((j + 1) * per, nh - 1), 0)))

    rs = [r for _, r in DIL_PAIRS]
    vec = lambda wd: _resident((1, 1, wd), (layer, 0, 0))
    wbr = _resident((1, ATTN_W, d), (layer, 0, 0))
    tile_f32 = lambda groups, rows: pltpu.VMEM((groups, rows, HEAD_DIM), F32)
    return pl.pallas_call(
        functools.partial(_merge_body, seq=s),
        out_shape=jax.ShapeDtypeStruct(x.shape, F32),
        grid=(b, s // tm),
        in_specs=[row(d), vec(d),
                  *[cls(r, ATTN_W) for r in rs], *[cls(r, HEAD_DIM) for r in rs],
                  *halo_specs(POOL_W), *halo_specs(2 * CONV_W),
                  _resident((1, len(POOL_SIZES), POOL_GROUP, POOL_GROUP), (layer, 0, 0, 0)),
                  vec(POOL_W),
                  _resident((1, CONV_K, CONV_W), (layer, 0, 0)),
                  vec(CONV_W), vec(CONV_W), vec(CONV_W),
                  pl.BlockSpec((1, d, 3 * d), lambda i, j: (layer, 0, PROJ_W // (3 * d)),
                               pipeline_mode=pl.Buffered(1)),
                  wbr, wbr, wbr, _resident((1, d, d), (layer, 0, 0))],
        out_specs=row(d),
        scratch_shapes=[pltpu.VMEM((tm, d), BF16),
                        pltpu.VMEM((tm, ATTN_W), BF16), pltpu.VMEM((tm, POOL_W), BF16),
                        pltpu.VMEM((tm, CONV_W), BF16),
                        pltpu.VMEM((3, tm, d), F32),
                        pltpu.VMEM((len(rs), HEADS_PER_GROUP, tm, HEAD_DIM), F32),
                        tile_f32(len(rs), tm),
                        tile_f32(len(POOL_SIZES), tm + 2 * HALO),
                        tile_f32(CONV_GROUPS, tm + 2 * HALO),
                        tile_f32(CONV_GROUPS, tm)],
        compiler_params=_params(2),
        name="merge_out",
    )(x, norm, *os_, *lses, xp, xp, xp, xc, xc, xc,
      w["pool_w"], w["pool_scale"], w["conv_dw"], w["conv_b"], w["conv_ln_g"], w["conv_ln_b"],
      w["w_in"], w["w_br_attn"], w["w_br_pool"], w["w_br_conv"], w["w_out"])


def _memkv_body(m_ref, g_ref, w_ref, k_ref, v_ref):
    mn = _rms(m_ref[0], g_ref[0]).astype(BF16)
    kv = jnp.dot(mn, w_ref[0], preferred_element_type=F32)
    k_ref[0] = kv[:, :X_W].astype(BF16)
    v_ref[0] = kv[:, X_W:].astype(BF16)


def _memkv(mem, norm, wkv, layer):
    b, m, d = mem.shape
    out = pl.BlockSpec((1, m, X_W), lambda i: (i, 0, 0))
    return pl.pallas_call(
        _memkv_body,
        out_shape=(jax.ShapeDtypeStruct((b, m, X_W), BF16),) * 2,
        grid=(b,),
        in_specs=[pl.BlockSpec((1, m, d), lambda i: (i, 0, 0)),
                  _resident((1, 1, d), (layer, 0, 0)),
                  _resident((1, d, 2 * X_W), (layer, 0, 0))],
        out_specs=(out, out),
        compiler_params=_params(1),
        name="mem_kv",
    )(mem, norm, wkv)


def _xattn_body(x_ref, g_ref, wq_ref, k_ref, v_ref, wo_ref, out_ref, q_scr, o_scr):
    xf = x_ref[0]
    xn = _rms(xf, g_ref[0]).astype(BF16)
    q_scr[...] = jnp.dot(xn, wq_ref[0], preferred_element_type=F32).astype(BF16)
    scale2 = HEAD_DIM ** -0.5 * LOG2_E
    for h in range(X_HEADS):
        cols = slice(h * HEAD_DIM, (h + 1) * HEAD_DIM)
        s = lax.dot_general(q_scr[:, cols], k_ref[0, :, cols], (((1,), (1,)), ((), ())),
                            preferred_element_type=F32) * scale2
        p = jnp.exp2(s - jnp.max(s, axis=-1, keepdims=True))
        l = jnp.sum(p, axis=-1, keepdims=True)
        o = jnp.dot(p.astype(BF16), v_ref[0, :, cols], preferred_element_type=F32) / l
        o_scr[:, cols] = o.astype(BF16)
    out_ref[0] = xf + jnp.dot(o_scr[...], wo_ref[0], preferred_element_type=F32)


def _xattn(x, norm, wq, kmem, vmem, wo, layer, *, tm=512):
    b, s, d = x.shape
    m = kmem.shape[1]
    tm = min(tm, s)
    row = pl.BlockSpec((1, tm, d), lambda i, j: (i, j, 0))
    mem = pl.BlockSpec((1, m, X_W), lambda i, j: (i, 0, 0))
    return pl.pallas_call(
        _xattn_body,
        out_shape=jax.ShapeDtypeStruct(x.shape, F32),
        grid=(b, s // tm),
        in_specs=[row, _resident((1, 1, d), (layer, 0, 0)), _resident((1, d, X_W), (layer, 0, 0)),
                  mem, mem, _resident((1, X_W, d), (layer, 0, 0))],
        out_specs=row,
        scratch_shapes=[pltpu.VMEM((tm, X_W), BF16), pltpu.VMEM((tm, X_W), BF16)],
        compiler_params=_params(2),
        name="mem_xattn",
    )(x, norm, wq, kmem, vmem, wo)


def _trunk(x, mem, w, biases, depth):
    for l in range(depth):
        x = _ffn(x, w["ffn1_norm"], w["ffn1_w_gu"], w["ffn1_w_down"], w["final_norm"], l, final=False)
        *qkvs, xp, xc = _proj(x, w["mix_norm"], w["w_in"], l)
        outs = [_attn_group(qkv, bias) for qkv, bias in zip(qkvs, biases)]
        x = _merge(x, w["mix_norm"], [o for o, _ in outs], [s for _, s in outs], xp, xc, w, l)
        kmem, vmem = _memkv(mem, w["mem_norm"], w["xattn_wkv"], l)
        x = _xattn(x, w["xattn_norm"], w["xattn_wq"], kmem, vmem, w["xattn_wo"], l)
        x = _ffn(x, w["ffn2_norm"], w["ffn2_w_gu"], w["ffn2_w_down"], w["final_norm"], l,
                 final=(l == depth - 1))
    return x


def kernel(x_prompt, x_sample, mem_prompt, mem_sample, rel_bias, ffn1_norm, ffn1_w_gu, ffn1_w_down, mix_norm, w_in, pool_w, pool_scale, conv_dw, conv_b, conv_ln_g, conv_ln_b, w_br_attn, w_br_pool, w_br_conv, w_out, xattn_norm, mem_norm, xattn_wq, xattn_wkv, xattn_wo, ffn2_norm, ffn2_w_gu, ffn2_w_down, final_norm):
    depth = w_in.shape[0]
    mat = lambda t: t.astype(BF16)
    vec = lambda t: t.reshape(depth, 1, t.shape[-1])
    w = dict(
        ffn1_norm=vec(ffn1_norm), ffn1_w_gu=mat(ffn1_w_gu), ffn1_w_down=mat(ffn1_w_down),
        mix_norm=vec(mix_norm), w_in=mat(w_in), pool_w=mat(pool_w), pool_scale=vec(pool_scale),
        conv_dw=conv_dw, conv_b=vec(conv_b), conv_ln_g=vec(conv_ln_g), conv_ln_b=vec(conv_ln_b),
        w_br_attn=mat(w_br_attn), w_br_pool=mat(w_br_pool), w_br_conv=mat(w_br_conv), w_out=mat(w_out),
        xattn_norm=vec(xattn_norm), mem_norm=vec(mem_norm), xattn_wq=mat(xattn_wq),
        xattn_wkv=mat(xattn_wkv), xattn_wo=mat(xattn_wo),
        ffn2_norm=vec(ffn2_norm), ffn2_w_gu=mat(ffn2_w_gu), ffn2_w_down=mat(ffn2_w_down),
        final_norm=final_norm.reshape(1, -1),
    )
    biases = [_band_bias(rel_bias, g, r) for g, (_, r) in enumerate(DIL_PAIRS)]
    y_prompt = _trunk(x_prompt, mem_prompt, w, biases, depth)
    y_sample = _trunk(x_sample, mem_sample, w, biases, depth)
    return (y_prompt, y_sample)
```

```python
import functools
import math

import jax
import jax.numpy as jnp
from jax import lax
from jax.experimental import pallas as pl
from jax.experimental.pallas import tpu as pltpu

F32 = jnp.float32
BF16 = jnp.bfloat16

D_MODEL = 1024
D_FF = 2816
DIL_PAIRS = ((128, 1), (512, 4), (2048, 16))
HEADS_PER_GROUP = 4
N_HEADS_A = len(DIL_PAIRS) * HEADS_PER_GROUP
HEAD_DIM = 128
QKV_W = N_HEADS_A * HEAD_DIM
ATTN_W = HEADS_PER_GROUP * HEAD_DIM
BAND_HALF = 64
REL_BUCKETS = 32
REL_MAX_EXACT = 8
REL_MAX_DIST = 1024
POOL_SIZES = (2, 4, 8, 16)
POOL_GROUP = 128
POOL_W = len(POOL_SIZES) * POOL_GROUP
CONV_W = 512
CONV_K = 31
CONV_GROUPS = CONV_W // HEAD_DIM
X_HEADS = 4
X_W = X_HEADS * HEAD_DIM
PROJ_W = 3 * QKV_W + POOL_W + 2 * CONV_W
EPS = 1e-6
NEG_INF = -1e30
LOG2_E = math.log2(math.e)
LN_2 = math.log(2.0)

HALO = 16
CONV_CHUNK = 256
GATE_CHUNK = 256
Q_SUB = 128
ATTN_TILE = 2048
VMEM_LIMIT = 56 * 1024 * 1024


def _params(n_axes):
    return pltpu.CompilerParams(dimension_semantics=("arbitrary",) * n_axes,
                                vmem_limit_bytes=VMEM_LIMIT)


def _resident(shape, index):
    return pl.BlockSpec(shape, lambda *_: index, pipeline_mode=pl.Buffered(1))


def _rms(xf, g):
    ms = jnp.mean(xf * xf, axis=-1, keepdims=True)
    return xf * lax.rsqrt(ms + EPS) * g


def _sigmoid(x):
    return 1.0 / (1.0 + jnp.exp(-x))


def _zero_after(v):
    bits = pltpu.bitcast(v, jnp.uint32)
    return pltpu.bitcast((bits >> 16) >> 16, F32)


def _ffn_body(x_ref, g_ref, wgu_ref, wd_ref, fg_ref, o_ref, xn_ref, act_ref, *, ck, final):
    xf = x_ref[0]
    xn_ref[...] = _rms(xf, g_ref[0]).astype(BF16)
    for c in range(D_FF // ck):
        xn = xn_ref[...]
        a = jnp.dot(xn, wgu_ref[0, :, c * ck:(c + 1) * ck], preferred_element_type=F32)
        u = jnp.dot(xn, wgu_ref[0, :, D_FF + c * ck:D_FF + (c + 1) * ck], preferred_element_type=F32)
        act_ref[:, c * ck:(c + 1) * ck] = (a * _sigmoid(a) * u).astype(BF16)
    y = xf + 0.5 * jnp.dot(act_ref[...], wd_ref[0], preferred_element_type=F32)
    if final:
        y = _rms(y, fg_ref[...])
    o_ref[0] = y


def _ffn(x, norm, w_gu, w_down, final_g, layer, *, final, tm=1024, ck=256):
    b, s, d = x.shape
    tm = min(tm, s)
    body = functools.partial(_ffn_body, ck=ck, final=final)
    return pl.pallas_call(
        body,
        out_shape=jax.ShapeDtypeStruct(x.shape, F32),
        grid=(b, s // tm),
        in_specs=[
            pl.BlockSpec((1, tm, d), lambda i, j: (i, j, 0)),
            _resident((1, 1, d), (layer, 0, 0)),
            _resident((1, d, 2 * D_FF), (layer, 0, 0)),
            _resident((1, D_FF, d), (layer, 0, 0)),
            _resident((1, d), (0, 0)),
        ],
        out_specs=pl.BlockSpec((1, tm, d), lambda i, j: (i, j, 0)),
        scratch_shapes=[pltpu.VMEM((tm, d), BF16), pltpu.VMEM((tm, D_FF), BF16)],
        compiler_params=_params(2),
        name="ffn",
    )(x, norm, w_gu, w_down, final_g)


def _proj_body(x_ref, g_ref, w_ref, qkv0_ref, qkv1_ref, qkv2_ref, xp_ref, xc_ref,
               xn_ref, xn4_ref, xn16_ref, xs_ref):
    tm = xn_ref.shape[0]
    xn = _rms(x_ref[0], g_ref[0])
    xn_ref[...] = xn.astype(BF16)

    def chunk(lhs_ref, base):
        return jnp.dot(lhs_ref[...], w_ref[0, :, base:base + ATTN_W], preferred_element_type=F32)

    for part in range(3):
        qkv0_ref[0, 0, :, part * ATTN_W:(part + 1) * ATTN_W] = chunk(xn_ref, part * QKV_W).astype(BF16)
    xp_ref[0] = chunk(xn_ref, 3 * QKV_W)
    for c in range(2):
        xc_ref[0, :, c * CONV_W:(c + 1) * CONV_W] = chunk(xn_ref, 3 * QKV_W + POOL_W + c * CONV_W)

    for t in range(xs_ref.shape[0]):
        xs_ref[t] = xn[:, t * HEAD_DIM:(t + 1) * HEAD_DIM]
    for g, (lhs_ref, out_ref) in ((1, (xn4_ref, qkv1_ref)), (2, (xn16_ref, qkv2_ref))):
        r = DIL_PAIRS[g][1]
        n = tm // r
        for c in range(r):
            for t in range(xs_ref.shape[0]):
                lhs_ref[c * n:(c + 1) * n, t * HEAD_DIM:(t + 1) * HEAD_DIM] = (
                    xs_ref[t, pl.ds(c, n, stride=r), :].astype(BF16))
        for part in range(3):
            y = chunk(lhs_ref, part * QKV_W + g * ATTN_W).astype(BF16)
            for c in range(r):
                out_ref[0, c, :, part * ATTN_W:(part + 1) * ATTN_W] = y[c * n:(c + 1) * n]


def _proj(x, norm, w_in, layer, *, tm=512):
    b, s, d = x.shape
    tm = min(tm, s)
    row = lambda w: pl.BlockSpec((1, tm, w), lambda i, j: (i, j, 0))
    rs = [r for _, r in DIL_PAIRS]
    qkv_shape = lambda r: jax.ShapeDtypeStruct((b, r, s // r, 3 * ATTN_W), BF16)
    qkv_spec = lambda r: pl.BlockSpec((1, r, tm // r, 3 * ATTN_W), lambda i, j: (i, 0, j, 0))
    return pl.pallas_call(
        _proj_body,
        out_shape=(*[qkv_shape(r) for r in rs],
                   jax.ShapeDtypeStruct((b, s, POOL_W), F32),
                   jax.ShapeDtypeStruct((b, s, 2 * CONV_W), F32)),
        grid=(b, s // tm),
        in_specs=[row(d), _resident((1, 1, d), (layer, 0, 0)),
                  _resident((1, d, PROJ_W), (layer, 0, 0))],
        out_specs=(*[qkv_spec(r) for r in rs], row(POOL_W), row(2 * CONV_W)),
        scratch_shapes=[pltpu.VMEM((tm, d), BF16), pltpu.VMEM((tm, d), BF16), pltpu.VMEM((tm, d), BF16),
                        pltpu.VMEM((d // HEAD_DIM, tm, HEAD_DIM), F32)],
        compiler_params=_params(2),
        name="in_proj",
    )(x, norm, w_in)


def _attn_body(q_ref, kp_ref, kc_ref, kn_ref, vp_ref, vc_ref, vn_ref, bias_ref,
               o_ref, lse_ref, kedge, vedge, *, tl, seq_l):
    h64 = BAND_HALF
    n_sub = tl // Q_SUB
    win = 2 * Q_SUB
    for edge, prev, cur, nxt in ((kedge, kp_ref, kc_ref, kn_ref), (vedge, vp_ref, vc_ref, vn_ref)):
        if n_sub == 1:
            edge[0, 0:h64] = prev[...]
            edge[0, h64:h64 + tl] = cur[...]
            edge[0, h64 + tl:win] = nxt[...]
        else:
            edge[0, 0:h64] = prev[...]
            edge[0, h64:win] = cur[0:win - h64]
            edge[1, 0:win - h64] = cur[tl - (win - h64):tl]
            edge[1, win - h64:win] = nxt[...]

    def window(edge, cur, j, cols):
        if j == 0:
            return edge[0, :, cols]
        if j == n_sub - 1:
            return edge[1, :, cols]
        return cur[j * Q_SUB - h64:j * Q_SUB - h64 + win, cols]

    row0 = pl.program_id(2) * tl
    scale2 = HEAD_DIM ** -0.5 * LOG2_E
    lane = lax.broadcasted_iota(jnp.int32, (Q_SUB, HEAD_DIM), 1)
    for j in range(n_sub):
        rows = slice(j * Q_SUB, (j + 1) * Q_SUB)
        kmask = None
        if j == 0 or j == n_sub - 1:
            kpos = row0 + (j * Q_SUB - h64) + lax.broadcasted_iota(jnp.int32, (1, win), 1)
            kmask = jnp.where((kpos >= 0) & (kpos < seq_l), 0.0, NEG_INF).astype(F32)
        lse_blk = jnp.zeros((Q_SUB, HEAD_DIM), F32)
        for h in range(HEADS_PER_GROUP):
            cols = slice(h * HEAD_DIM, (h + 1) * HEAD_DIM)
            q = q_ref[rows, cols]
            k = window(kedge, kc_ref, j, cols)
            v = window(vedge, vc_ref, j, cols)
            s = lax.dot_general(q, k, (((1,), (1,)), ((), ())), preferred_element_type=F32)
            s = s * scale2 + bias_ref[h]
            if kmask is not None:
                s = s + kmask
            mx = jnp.max(s, axis=-1, keepdims=True)
            p = jnp.exp2(s - mx)
            l = jnp.sum(p, axis=-1, keepdims=True)
            o = jnp.dot(p.astype(BF16), v, preferred_element_type=F32) / l
            o_ref[rows, cols] = o.astype(o_ref.dtype)
            lse_blk = jnp.where(lane == h, (mx + jnp.log2(l)) * LN_2, lse_blk)
        lse_ref[rows, :] = lse_blk


def _attn_group(qkv, bias):
    b, r, sl, _ = qkv.shape
    tl = min(ATTN_TILE, sl)
    nh = sl // BAND_HALF
    per = tl // BAND_HALF
    cur = lambda part: pl.BlockSpec((None, None, tl, ATTN_W), lambda i, c, m: (i, c, m, part))
    prev = lambda part: pl.BlockSpec((None, None, BAND_HALF, ATTN_W),
                                     lambda i, c, m: (i, c, jnp.maximum(m * per - 1, 0), part))
    nxt = lambda part: pl.BlockSpec((None, None, BAND_HALF, ATTN_W),
                                    lambda i, c, m: (i, c, jnp.minimum((m + 1) * per, nh - 1), part))
    return pl.pallas_call(
        functools.partial(_attn_body, tl=tl, seq_l=sl),
        out_shape=(jax.ShapeDtypeStruct((b, r, sl, ATTN_W), BF16),
                   jax.ShapeDtypeStruct((b, r, sl, HEAD_DIM), F32)),
        grid=(b, r, sl // tl),
        in_specs=[cur(0), prev(1), cur(1), nxt(1), prev(2), cur(2), nxt(2),
                  _resident((HEADS_PER_GROUP, Q_SUB, 2 * Q_SUB), (0, 0, 0))],
        out_specs=(pl.BlockSpec((None, None, tl, ATTN_W), lambda i, c, m: (i, c, m, 0)),
                   pl.BlockSpec((None, None, tl, HEAD_DIM), lambda i, c, m: (i, c, m, 0))),
        scratch_shapes=[pltpu.VMEM((2, 2 * Q_SUB, ATTN_W), BF16),
                        pltpu.VMEM((2, 2 * Q_SUB, ATTN_W), BF16)],
        compiler_params=_params(3),
        name=f"dilated_attn_r{r}",
    )(qkv, qkv, qkv, qkv, qkv, qkv, qkv, bias)


def _t5_bucket(rel):
    half = REL_BUCKETS // 2
    n = jnp.abs(rel)
    nf = jnp.maximum(n, 1).astype(F32)
    large = REL_MAX_EXACT + (jnp.log(nf / REL_MAX_EXACT) / math.log(REL_MAX_DIST / REL_MAX_EXACT)
                             * (half - REL_MAX_EXACT)).astype(jnp.int32)
    large = jnp.minimum(large, half - 1)
    return jnp.where(rel > 0, half, 0) + jnp.where(n < REL_MAX_EXACT, n, large)


def _band_bias(rel_bias, g, r):
    delta = jnp.arange(2 * Q_SUB)[None, :] - BAND_HALF - jnp.arange(Q_SUB)[:, None]
    bucket = _t5_bucket(r * delta)
    table = rel_bias[:, g * HEADS_PER_GROUP:(g + 1) * HEADS_PER_GROUP].astype(F32)
    onehot = bucket[None, :, :, None] == jnp.arange(REL_BUCKETS)
    tbl = jnp.sum(jnp.where(onehot, table.T[:, None, None, :], 0.0), axis=-1)
    return jnp.where((jnp.abs(delta) <= BAND_HALF)[None], tbl * LOG2_E, NEG_INF)


def _pool_conv_steps(xp_p, xp_c, xp_n, xc_p, xc_c, xc_n, pw_ref, ps_ref, dw_ref, cb_ref, lg_ref, lb_ref,
                     yb_ref, yc_ref, pbuf, ubuf, sbuf, *, ts, seq, pace):
    i = pl.program_id(1)
    has_prev = (i > 0).astype(F32)
    has_next = (i < pl.num_programs(1) - 1).astype(F32)
    chunk = min(CONV_CHUNK, ts)
    half = chunk // 2
    steps = []

    def fill(buf, g, prev, cur, nxt):
        buf[g, 0:HALO] = prev * has_prev
        buf[g, HALO:HALO + ts] = cur
        buf[g, HALO + ts:HALO + ts + HALO] = nxt * has_next

    def shifted(buf, g, row0, off, par):
        return buf[g, pl.ds(HALO + row0 + off + par, half, stride=2), :]

    def pool_group(g, kw):
        cols = slice(g * POOL_GROUP, (g + 1) * POOL_GROUP)
        fill(pbuf, g, xp_p[0, :, cols], xp_c[0, :, cols], xp_n[0, :, cols])
        for row0 in range(0, ts, chunk):
            for par in range(2):
                acc = shifted(pbuf, g, row0, -(kw // 2), par)
                for dlt in range(-(kw // 2) + 1, kw // 2):
                    acc = acc + shifted(pbuf, g, row0, dlt, par)
                sbuf[g, pl.ds(row0 + par, half, stride=2), :] = acc
        pos = i * ts + lax.broadcasted_iota(jnp.int32, (ts, 1), 0)
        cnt = (jnp.minimum(pos + kw // 2, seq) - jnp.maximum(pos - kw // 2, 0)).astype(F32)
        pooled = (sbuf[g] / cnt - xp_c[0, :, cols]).astype(BF16)
        mixed = jnp.dot(pooled, pw_ref[0, g], preferred_element_type=F32)
        yb_ref[:, cols] = (mixed * ps_ref[0, :, cols]).astype(yb_ref.dtype)

    for g, kw in enumerate(POOL_SIZES):
        steps.append(functools.partial(pool_group, g, kw))

    def glu(ref, g):
        return ref[0, :, g * HEAD_DIM:(g + 1) * HEAD_DIM] * _sigmoid(
            ref[0, :, CONV_W + g * HEAD_DIM:CONV_W + (g + 1) * HEAD_DIM])

    def conv_fill(g):
        fill(ubuf, g, glu(xc_p, g), glu(xc_c, g), glu(xc_n, g))

    def conv_pass(g, row0, par):
        cols = slice(g * HEAD_DIM, (g + 1) * HEAD_DIM)
        bias = cb_ref[0, :, cols]
        after = pace()
        if after is not None:
            bias = bias + _zero_after(after)
        acc = jnp.zeros((half, HEAD_DIM), F32) + bias
        for kk in range(CONV_K):
            acc = acc + dw_ref[0, kk:kk + 1, cols] * shifted(ubuf, g, row0, kk - CONV_K // 2, par)
        sbuf[g, pl.ds(row0 + par, half, stride=2), :] = acc

    def conv_norm():
        tot = sum(jnp.sum(sbuf[g], axis=-1, keepdims=True) for g in range(CONV_GROUPS))
        mu = tot / CONV_W
        sq = sum(jnp.sum((sbuf[g] - mu) ** 2, axis=-1, keepdims=True) for g in range(CONV_GROUPS))
        inv = lax.rsqrt(sq / CONV_W + EPS)
        for g in range(CONV_GROUPS):
            cols = slice(g * HEAD_DIM, (g + 1) * HEAD_DIM)
            y = (sbuf[g] - mu) * inv * lg_ref[0, :, cols] + lb_ref[0, :, cols]
            yc_ref[:, cols] = (y * _sigmoid(y)).astype(yc_ref.dtype)

    for g in range(CONV_GROUPS):
        steps.append(functools.partial(conv_fill, g))
        for row0 in range(0, ts, chunk):
            for par in range(2):
                steps.append(functools.partial(conv_pass, g, row0, par))
    steps.append(conv_norm)
    return steps


def _emit_interleaved(first, second):
    order = sorted([((k + 1) / len(first), 0, k) for k in range(len(first))]
                   + [((k + 0.5) / len(second), 1, k) for k in range(len(second))])
    for _, which, k in order:
        (first, second)[which][k]()


def _merge_body(x_ref, g_ref, o0, o1, o2, l0, l1, l2, xp_p, xp_c, xp_n, xc_p, xc_c, xc_n,
                pw_ref, ps_ref, dw_ref, cb_ref, lg_ref, lb_ref,
                wg_ref, wa_ref, wb_ref, wc_ref, wo_ref, out_ref,
                xn_ref, at_ref, yb_ref, yc_ref, gate_ref, obuf, lbuf, pbuf, ubuf, sbuf, *, seq):
    tm = xn_ref.shape[0]
    d = D_MODEL
    n_g = len(DIL_PAIRS)
    xn_ref[...] = _rms(x_ref[0], g_ref[0]).astype(BF16)

    def attn_head(h):
        for gi, (o, l, (_, r)) in enumerate(zip((o0, o1, o2), (l0, l1, l2), DIL_PAIRS)):
            for c in range(r):
                dst = pl.ds(c, tm // r, stride=r) if r > 1 else pl.ds(0, tm)
                if h == 0:
                    lbuf[gi, dst, :] = l[0, c]
                obuf[gi, h, dst, :] = o[0, c, :, h * HEAD_DIM:(h + 1) * HEAD_DIM].astype(F32)
        ls = [lbuf[gi, :, h:h + 1] for gi in range(n_g)]
        mx = jnp.maximum(jnp.maximum(ls[0], ls[1]), ls[2])
        es = [jnp.exp(t - mx) for t in ls]
        den = es[0] + es[1] + es[2]
        acc = sum((es[gi] / den) * obuf[gi, h] for gi in range(n_g))
        at_ref[:, h * HEAD_DIM:(h + 1) * HEAD_DIM] = acc.astype(BF16)

    last_gate = [None]

    def gate_chunk(bi, c):
        z = jnp.dot(xn_ref[...], wg_ref[0, :, bi * d + c * GATE_CHUNK:bi * d + (c + 1) * GATE_CHUNK],
                    preferred_element_type=F32)
        gate_ref[bi, :, c * GATE_CHUNK:(c + 1) * GATE_CHUNK] = z
        last_gate[0] = z[tm - 1:tm, GATE_CHUNK - HEAD_DIM:GATE_CHUNK]

    vpu_steps = [functools.partial(attn_head, h) for h in range(HEADS_PER_GROUP)]
    vpu_steps += _pool_conv_steps(xp_p, xp_c, xp_n, xc_p, xc_c, xc_n, pw_ref, ps_ref, dw_ref, cb_ref,
                                  lg_ref, lb_ref, yb_ref, yc_ref, pbuf, ubuf, sbuf, ts=tm, seq=seq,
                                  pace=lambda: last_gate[0])
    mxu_steps = [functools.partial(gate_chunk, bi, c) for bi in range(3) for c in range(d // GATE_CHUNK)]
    _emit_interleaved(vpu_steps, mxu_steps)

    merged = None
    for bi, (br, w) in enumerate(((at_ref, wa_ref), (yb_ref, wb_ref), (yc_ref, wc_ref))):
        y = _sigmoid(gate_ref[bi]) * jnp.dot(br[...], w[0], preferred_element_type=F32)
        merged = y if merged is None else merged + y
    out_ref[0] = x_ref[0] + jnp.dot(merged.astype(BF16), wo_ref[0], preferred_element_type=F32)


def _merge(x, norm, os_, lses, xp, xc, w, layer, *, tm=512):
    b, s, d = x.shape
    tm = min(tm, s)
    per = tm // HALO
    nh = s // HALO
    row = lambda wd: pl.BlockSpec((1, tm, wd), lambda i, j: (i, j, 0))
    cls = lambda r, wd: pl.BlockSpec((1, r, tm // r, wd), lambda i, j: (i, 0, j, 0))

    def halo_specs(wd):
        return (pl.BlockSpec((1, HALO, wd), lambda i, j: (i, jnp.maximum(j * per - 1, 0), 0)),
                row(wd),
                pl.BlockSpec((1, HALO, wd), lambda i, j: (i, jnp.minimum((j + 1) * per, nh - 1), 0)))

    rs = [r for _, r in DIL_PAIRS]
    vec = lambda wd: _resident((1, 1, wd), (layer, 0, 0))
    wbr = _resident((1, ATTN_W, d), (layer, 0, 0))
    tile_f32 = lambda groups, rows: pltpu.VMEM((groups, rows, HEAD_DIM), F32)
    return pl.pallas_call(
        functools.partial(_merge_body, seq=s),
        out_shape=jax.ShapeDtypeStruct(x.shape, F32),
        grid=(b, s // tm),
        in_specs=[row(d), vec(d),
                  *[cls(r, ATTN_W) for r in rs], *[cls(r, HEAD_DIM) for r in rs],
                  *halo_specs(POOL_W), *halo_specs(2 * CONV_W),
                  _resident((1, len(POOL_SIZES), POOL_GROUP, POOL_GROUP), (layer, 0, 0, 0)),
                  vec(POOL_W),
                  _resident((1, CONV_K, CONV_W), (layer, 0, 0)),
                  vec(CONV_W), vec(CONV_W), vec(CONV_W),
                  pl.BlockSpec((1, d, 3 * d), lambda i, j: (layer, 0, PROJ_W // (3 * d)),
                               pipeline_mode=pl.Buffered(1)),
                  wbr, wbr, wbr, _resident((1, d, d), (layer, 0, 0))],
        out_specs=row(d),
        scratch_shapes=[pltpu.VMEM((tm, d), BF16),
                        pltpu.VMEM((tm, ATTN_W), BF16), pltpu.VMEM((tm, POOL_W), BF16),
                        pltpu.VMEM((tm, CONV_W), BF16),
                        pltpu.VMEM((3, tm, d), F32),
                        pltpu.VMEM((len(rs), HEADS_PER_GROUP, tm, HEAD_DIM), F32),
                        tile_f32(len(rs), tm),
                        tile_f32(len(POOL_SIZES), tm + 2 * HALO),
                        tile_f32(CONV_GROUPS, tm + 2 * HALO),
                        tile_f32(CONV_GROUPS, tm)],
        compiler_params=_params(2),
        name="merge_out",
    )(x, norm, *os_, *lses, xp, xp, xp, xc, xc, xc,
      w["pool_w"], w["pool_scale"], w["conv_dw"], w["conv_b"], w["conv_ln_g"], w["conv_ln_b"],
      w["w_in"], w["w_br_attn"], w["w_br_pool"], w["w_br_conv"], w["w_out"])


def _memkv_body(m_ref, g_ref, w_ref, k_ref, v_ref):
    mn = _rms(m_ref[0], g_ref[0]).astype(BF16)
    kv = jnp.dot(mn, w_ref[0], preferred_element_type=F32)
    k_ref[0] = kv[:, :X_W].astype(BF16)
    v_ref[0] = kv[:, X_W:].astype(BF16)


def _memkv(mem, norm, wkv, layer):
    b, m, d = mem.shape
    out = pl.BlockSpec((1, m, X_W), lambda i: (i, 0, 0))
    return pl.pallas_call(
        _memkv_body,
        out_shape=(jax.ShapeDtypeStruct((b, m, X_W), BF16),) * 2,
        grid=(b,),
        in_specs=[pl.BlockSpec((1, m, d), lambda i: (i, 0, 0)),
                  _resident((1, 1, d), (layer, 0, 0)),
                  _resident((1, d, 2 * X_W), (layer, 0, 0))],
        out_specs=(out, out),
        compiler_params=_params(1),
        name="mem_kv",
    )(mem, norm, wkv)


def _xattn_body(x_ref, g_ref, wq_ref, k_ref, v_ref, wo_ref, out_ref, q_scr, o_scr):
    xf = x_ref[0]
    xn = _rms(xf, g_ref[0]).astype(BF16)
    q_scr[...] = jnp.dot(xn, wq_ref[0], preferred_element_type=F32).astype(BF16)
    scale2 = HEAD_DIM ** -0.5 * LOG2_E
    for h in range(X_HEADS):
        cols = slice(h * HEAD_DIM, (h + 1) * HEAD_DIM)
        s = lax.dot_general(q_scr[:, cols], k_ref[0, :, cols], (((1,), (1,)), ((), ())),
                            preferred_element_type=F32) * scale2
        p = jnp.exp2(s - jnp.max(s, axis=-1, keepdims=True))
        l = jnp.sum(p, axis=-1, keepdims=True)
        o = jnp.dot(p.astype(BF16), v_ref[0, :, cols], preferred_element_type=F32) / l
        o_scr[:, cols] = o.astype(BF16)
    out_ref[0] = xf + jnp.dot(o_scr[...], wo_ref[0], preferred_element_type=F32)


def _xattn(x, norm, wq, kmem, vmem, wo, layer, *, tm=1024):
    b, s, d = x.shape
    m = kmem.shape[1]
    tm = min(tm, s)
    row = pl.BlockSpec((1, tm, d), lambda i, j: (i, j, 0))
    mem = pl.BlockSpec((1, m, X_W), lambda i, j: (i, 0, 0))
    return pl.pallas_call(
        _xattn_body,
        out_shape=jax.ShapeDtypeStruct(x.shape, F32),
        grid=(b, s // tm),
        in_specs=[row, _resident((1, 1, d), (layer, 0, 0)), _resident((1, d, X_W), (layer, 0, 0)),
                  mem, mem, _resident((1, X_W, d), (layer, 0, 0))],
        out_specs=row,
        scratch_shapes=[pltpu.VMEM((tm, X_W), BF16), pltpu.VMEM((tm, X_W), BF16)],
        compiler_params=_params(2),
        name="mem_xattn",
    )(x, norm, wq, kmem, vmem, wo)


def _trunk(x, mem, w, biases, depth):
    for l in range(depth):
        x = _ffn(x, w["ffn1_norm"], w["ffn1_w_gu"], w["ffn1_w_down"], w["final_norm"], l, final=False)
        *qkvs, xp, xc = _proj(x, w["mix_norm"], w["w_in"], l)
        outs = [_attn_group(qkv, bias) for qkv, bias in zip(qkvs, biases)]
        x = _merge(x, w["mix_norm"], [o for o, _ in outs], [s for _, s in outs], xp, xc, w, l)
        kmem, vmem = _memkv(mem, w["mem_norm"], w["xattn_wkv"], l)
        x = _xattn(x, w["xattn_norm"], w["xattn_wq"], kmem, vmem, w["xattn_wo"], l)
        x = _ffn(x, w["ffn2_norm"], w["ffn2_w_gu"], w["ffn2_w_down"], w["final_norm"], l,
                 final=(l == depth - 1))
    return x


def kernel(x_prompt, x_sample, mem_prompt, mem_sample, rel_bias, ffn1_norm, ffn1_w_gu, ffn1_w_down, mix_norm, w_in, pool_w, pool_scale, conv_dw, conv_b, conv_ln_g, conv_ln_b, w_br_attn, w_br_pool, w_br_conv, w_out, xattn_norm, mem_norm, xattn_wq, xattn_wkv, xattn_wo, ffn2_norm, ffn2_w_gu, ffn2_w_down, final_norm):
    depth = w_in.shape[0]
    mat = lambda t: t.astype(BF16)
    vec = lambda t: t.reshape(depth, 1, t.shape[-1])
    w = dict(
        ffn1_norm=vec(ffn1_norm), ffn1_w_gu=mat(ffn1_w_gu), ffn1_w_down=mat(ffn1_w_down),
        mix_norm=vec(mix_norm), w_in=mat(w_in), pool_w=mat(pool_w), pool_scale=vec(pool_scale),
        conv_dw=conv_dw, conv_b=vec(conv_b), conv_ln_g=vec(conv_ln_g), conv_ln_b=vec(conv_ln_b),
        w_br_attn=mat(w_br_attn), w_br_pool=mat(w_br_pool), w_br_conv=mat(w_br_conv), w_out=mat(w_out),
        xattn_norm=vec(xattn_norm), mem_norm=vec(mem_norm), xattn_wq=mat(xattn_wq),
        xattn_wkv=mat(xattn_wkv), xattn_wo=mat(xattn_wo),
        ffn2_norm=vec(ffn2_norm), ffn2_w_gu=mat(ffn2_w_gu), ffn2_w_down=mat(ffn2_w_down),
        final_norm=final_norm.reshape(1, -1),
    )
    biases = [_band_bias(rel_bias, g, r) for g, (_, r) in enumerate(DIL_PAIRS)]
    y_prompt = _trunk(x_prompt, mem_prompt, w, biases, depth)
    y_sample = _trunk(x_sample, mem_sample, w, biases, depth)
    return (y_prompt, y_sample)
```

```python
import functools
import math

import jax
import jax.numpy as jnp
from jax import lax
from jax.experimental import pallas as pl
from jax.experimental.pallas import tpu as pltpu

F32 = jnp.float32
BF16 = jnp.bfloat16

D_MODEL = 1024
D_FF = 2816
DIL_PAIRS = ((128, 1), (512, 4), (2048, 16))
HEADS_PER_GROUP = 4
N_HEADS_A = len(DIL_PAIRS) * HEADS_PER_GROUP
HEAD_DIM = 128
QKV_W = N_HEADS_A * HEAD_DIM
ATTN_W = HEADS_PER_GROUP * HEAD_DIM
BAND_HALF = 64
REL_BUCKETS = 32
REL_MAX_EXACT = 8
REL_MAX_DIST = 1024
POOL_SIZES = (2, 4, 8, 16)
POOL_GROUP = 128
POOL_W = len(POOL_SIZES) * POOL_GROUP
CONV_W = 512
CONV_K = 31
CONV_GROUPS = CONV_W // HEAD_DIM
X_HEADS = 4
X_W = X_HEADS * HEAD_DIM
PROJ_W = 3 * QKV_W + POOL_W + 2 * CONV_W
EPS = 1e-6
NEG_INF = -1e30
LOG2_E = math.log2(math.e)
LN_2 = math.log(2.0)

HALO = 16
CONV_CHUNK = 256
ROW_PHASES = 4
GATE_CHUNK = 256
Q_SUB = 128
ATTN_TILE = 2048
VMEM_LIMIT = 56 * 1024 * 1024


def _params(n_axes):
    return pltpu.CompilerParams(dimension_semantics=("arbitrary",) * n_axes,
                                vmem_limit_bytes=VMEM_LIMIT)


def _resident(shape, index):
    return pl.BlockSpec(shape, lambda *_: index, pipeline_mode=pl.Buffered(1))


def _rms(xf, g):
    ms = jnp.mean(xf * xf, axis=-1, keepdims=True)
    return xf * lax.rsqrt(ms + EPS) * g


def _sigmoid(x):
    return 1.0 / (1.0 + jnp.exp(-x))


def _zero_after(v):
    bits = pltpu.bitcast(v, jnp.uint32)
    return pltpu.bitcast((bits >> 16) >> 16, F32)


def _ffn_body(x_ref, g_ref, wgu_ref, wd_ref, fg_ref, o_ref, xn_ref, act_ref, *, ck, final):
    xf = x_ref[0]
    xn_ref[...] = _rms(xf, g_ref[0]).astype(BF16)
    for c in range(D_FF // ck):
        xn = xn_ref[...]
        a = jnp.dot(xn, wgu_ref[0, :, c * ck:(c + 1) * ck], preferred_element_type=F32)
        u = jnp.dot(xn, wgu_ref[0, :, D_FF + c * ck:D_FF + (c + 1) * ck], preferred_element_type=F32)
        act_ref[:, c * ck:(c + 1) * ck] = (a * _sigmoid(a) * u).astype(BF16)
    y = xf + 0.5 * jnp.dot(act_ref[...], wd_ref[0], preferred_element_type=F32)
    if final:
        y = _rms(y, fg_ref[...])
    o_ref[0] = y


def _ffn(x, norm, w_gu, w_down, final_g, layer, *, final, tm=1024, ck=256):
    b, s, d = x.shape
    tm = min(tm, s)
    body = functools.partial(_ffn_body, ck=ck, final=final)
    return pl.pallas_call(
        body,
        out_shape=jax.ShapeDtypeStruct(x.shape, F32),
        grid=(b, s // tm),
        in_specs=[
            pl.BlockSpec((1, tm, d), lambda i, j: (i, j, 0)),
            _resident((1, 1, d), (layer, 0, 0)),
            _resident((1, d, 2 * D_FF), (layer, 0, 0)),
            _resident((1, D_FF, d), (layer, 0, 0)),
            _resident((1, d), (0, 0)),
        ],
        out_specs=pl.BlockSpec((1, tm, d), lambda i, j: (i, j, 0)),
        scratch_shapes=[pltpu.VMEM((tm, d), BF16), pltpu.VMEM((tm, D_FF), BF16)],
        compiler_params=_params(2),
        name="ffn",
    )(x, norm, w_gu, w_down, final_g)


def _proj_body(x_ref, g_ref, w_ref, qkv0_ref, qkv1_ref, qkv2_ref, xp_ref, xc_ref,
               xn_ref, xn4_ref, xn16_ref, xs_ref):
    tm = xn_ref.shape[0]
    xn = _rms(x_ref[0], g_ref[0])
    xn_ref[...] = xn.astype(BF16)

    def chunk(lhs_ref, base):
        return jnp.dot(lhs_ref[...], w_ref[0, :, base:base + ATTN_W], preferred_element_type=F32)

    for part in range(3):
        qkv0_ref[0, 0, :, part * ATTN_W:(part + 1) * ATTN_W] = chunk(xn_ref, part * QKV_W).astype(BF16)
    xp_ref[0] = chunk(xn_ref, 3 * QKV_W)
    xc_ref[0] = chunk(xn_ref, 3 * QKV_W + POOL_W) * _sigmoid(chunk(xn_ref, 3 * QKV_W + POOL_W + CONV_W))

    for t in range(xs_ref.shape[0]):
        xs_ref[t] = xn[:, t * HEAD_DIM:(t + 1) * HEAD_DIM]
    for g, (lhs_ref, out_ref) in ((1, (xn4_ref, qkv1_ref)), (2, (xn16_ref, qkv2_ref))):
        r = DIL_PAIRS[g][1]
        n = tm // r
        for c in range(r):
            for t in range(xs_ref.shape[0]):
                lhs_ref[c * n:(c + 1) * n, t * HEAD_DIM:(t + 1) * HEAD_DIM] = (
                    xs_ref[t, pl.ds(c, n, stride=r), :].astype(BF16))
        for part in range(3):
            y = chunk(lhs_ref, part * QKV_W + g * ATTN_W).astype(BF16)
            for c in range(r):
                out_ref[0, c, :, part * ATTN_W:(part + 1) * ATTN_W] = y[c * n:(c + 1) * n]


def _proj(x, norm, w_in, layer, *, tm=512):
    b, s, d = x.shape
    tm = min(tm, s)
    row = lambda w: pl.BlockSpec((1, tm, w), lambda i, j: (i, j, 0))
    rs = [r for _, r in DIL_PAIRS]
    qkv_shape = lambda r: jax.ShapeDtypeStruct((b, r, s // r, 3 * ATTN_W), BF16)
    qkv_spec = lambda r: pl.BlockSpec((1, r, tm // r, 3 * ATTN_W), lambda i, j: (i, 0, j, 0))
    return pl.pallas_call(
        _proj_body,
        out_shape=(*[qkv_shape(r) for r in rs],
                   jax.ShapeDtypeStruct((b, s, POOL_W), F32),
                   jax.ShapeDtypeStruct((b, s, CONV_W), F32)),
        grid=(b, s // tm),
        in_specs=[row(d), _resident((1, 1, d), (layer, 0, 0)),
                  _resident((1, d, PROJ_W), (layer, 0, 0))],
        out_specs=(*[qkv_spec(r) for r in rs], row(POOL_W), row(CONV_W)),
        scratch_shapes=[pltpu.VMEM((tm, d), BF16), pltpu.VMEM((tm, d), BF16), pltpu.VMEM((tm, d), BF16),
                        pltpu.VMEM((d // HEAD_DIM, tm, HEAD_DIM), F32)],
        compiler_params=_params(2),
        name="in_proj",
    )(x, norm, w_in)


def _attn_body(q_ref, kp_ref, kc_ref, kn_ref, vp_ref, vc_ref, vn_ref, bias_ref,
               o_ref, lse_ref, kedge, vedge, *, tl, seq_l):
    h64 = BAND_HALF
    n_sub = tl // Q_SUB
    win = 2 * Q_SUB
    for edge, prev, cur, nxt in ((kedge, kp_ref, kc_ref, kn_ref), (vedge, vp_ref, vc_ref, vn_ref)):
        if n_sub == 1:
            edge[0, 0:h64] = prev[...]
            edge[0, h64:h64 + tl] = cur[...]
            edge[0, h64 + tl:win] = nxt[...]
        else:
            edge[0, 0:h64] = prev[...]
            edge[0, h64:win] = cur[0:win - h64]
            edge[1, 0:win - h64] = cur[tl - (win - h64):tl]
            edge[1, win - h64:win] = nxt[...]

    def window(edge, cur, j, cols):
        if j == 0:
            return edge[0, :, cols]
        if j == n_sub - 1:
            return edge[1, :, cols]
        return cur[j * Q_SUB - h64:j * Q_SUB - h64 + win, cols]

    row0 = pl.program_id(2) * tl
    scale2 = HEAD_DIM ** -0.5 * LOG2_E
    lane = lax.broadcasted_iota(jnp.int32, (Q_SUB, HEAD_DIM), 1)
    for j in range(n_sub):
        rows = slice(j * Q_SUB, (j + 1) * Q_SUB)
        kmask = None
        if j == 0 or j == n_sub - 1:
            kpos = row0 + (j * Q_SUB - h64) + lax.broadcasted_iota(jnp.int32, (1, win), 1)
            kmask = jnp.where((kpos >= 0) & (kpos < seq_l), 0.0, NEG_INF).astype(F32)
        lse_blk = jnp.zeros((Q_SUB, HEAD_DIM), F32)
        for h in range(HEADS_PER_GROUP):
            cols = slice(h * HEAD_DIM, (h + 1) * HEAD_DIM)
            q = q_ref[rows, cols]
            k = window(kedge, kc_ref, j, cols)
            v = window(vedge, vc_ref, j, cols)
            s = lax.dot_general(q, k, (((1,), (1,)), ((), ())), preferred_element_type=F32)
            s = s * scale2 + bias_ref[h]
            if kmask is not None:
                s = s + kmask
            mx = jnp.max(s, axis=-1, keepdims=True)
            p = jnp.exp2(s - mx)
            l = jnp.sum(p, axis=-1, keepdims=True)
            o = jnp.dot(p.astype(BF16), v, preferred_element_type=F32) / l
            o_ref[rows, cols] = o.astype(o_ref.dtype)
            lse_blk = jnp.where(lane == h, (mx + jnp.log2(l)) * LN_2, lse_blk)
        lse_ref[rows, :] = lse_blk


def _attn_group(qkv, bias):
    b, r, sl, _ = qkv.shape
    tl = min(ATTN_TILE, sl)
    nh = sl // BAND_HALF
    per = tl // BAND_HALF
    cur = lambda part: pl.BlockSpec((None, None, tl, ATTN_W), lambda i, c, m: (i, c, m, part))
    prev = lambda part: pl.BlockSpec((None, None, BAND_HALF, ATTN_W),
                                     lambda i, c, m: (i, c, jnp.maximum(m * per - 1, 0), part))
    nxt = lambda part: pl.BlockSpec((None, None, BAND_HALF, ATTN_W),
                                    lambda i, c, m: (i, c, jnp.minimum((m + 1) * per, nh - 1), part))
    return pl.pallas_call(
        functools.partial(_attn_body, tl=tl, seq_l=sl),
        out_shape=(jax.ShapeDtypeStruct((b, r, sl, ATTN_W), BF16),
                   jax.ShapeDtypeStruct((b, r, sl, HEAD_DIM), F32)),
        grid=(b, r, sl // tl),
        in_specs=[cur(0), prev(1), cur(1), nxt(1), prev(2), cur(2), nxt(2),
                  _resident((HEADS_PER_GROUP, Q_SUB, 2 * Q_SUB), (0, 0, 0))],
        out_specs=(pl.BlockSpec((None, None, tl, ATTN_W), lambda i, c, m: (i, c, m, 0)),
                   pl.BlockSpec((None, None, tl, HEAD_DIM), lambda i, c, m: (i, c, m, 0))),
        scratch_shapes=[pltpu.VMEM((2, 2 * Q_SUB, ATTN_W), BF16),
                        pltpu.VMEM((2, 2 * Q_SUB, ATTN_W), BF16)],
        compiler_params=_params(3),
        name=f"dilated_attn_r{r}",
    )(qkv, qkv, qkv, qkv, qkv, qkv, qkv, bias)


def _t5_bucket(rel):
    half = REL_BUCKETS // 2
    n = jnp.abs(rel)
    nf = jnp.maximum(n, 1).astype(F32)
    large = REL_MAX_EXACT + (jnp.log(nf / REL_MAX_EXACT) / math.log(REL_MAX_DIST / REL_MAX_EXACT)
                             * (half - REL_MAX_EXACT)).astype(jnp.int32)
    large = jnp.minimum(large, half - 1)
    return jnp.where(rel > 0, half, 0) + jnp.where(n < REL_MAX_EXACT, n, large)


def _band_bias(rel_bias, g, r):
    delta = jnp.arange(2 * Q_SUB)[None, :] - BAND_HALF - jnp.arange(Q_SUB)[:, None]
    bucket = _t5_bucket(r * delta)
    table = rel_bias[:, g * HEADS_PER_GROUP:(g + 1) * HEADS_PER_GROUP].astype(F32)
    onehot = bucket[None, :, :, None] == jnp.arange(REL_BUCKETS)
    tbl = jnp.sum(jnp.where(onehot, table.T[:, None, None, :], 0.0), axis=-1)
    return jnp.where((jnp.abs(delta) <= BAND_HALF)[None], tbl * LOG2_E, NEG_INF)


def _pool_conv_steps(xp_p, xp_c, xp_n, xc_p, xc_c, xc_n, pw_ref, ps_ref, dw_ref, cb_ref, lg_ref, lb_ref,
                     yb_ref, yc_ref, pbuf, ubuf, sbuf, *, ts, seq, pace):
    i = pl.program_id(1)
    has_prev = (i > 0).astype(F32)
    has_next = (i < pl.num_programs(1) - 1).astype(F32)
    chunk = min(CONV_CHUNK, ts)
    quarter = chunk // ROW_PHASES
    steps = []

    def fill(buf, g, prev, cur, nxt):
        buf[g, 0:HALO] = prev * has_prev
        buf[g, HALO:HALO + ts] = cur
        buf[g, HALO + ts:HALO + ts + HALO] = nxt * has_next

    def rows_from(buf, g, start):
        return buf[g, pl.ds(HALO + start, quarter, stride=ROW_PHASES), :]

    def windowed(buf, g, row0, first, taps, weight):
        accs = [None] * ROW_PHASES
        for m in range(taps + ROW_PHASES - 1):
            v = rows_from(buf, g, row0 + first + m)
            for ph in range(ROW_PHASES):
                k = m - ph
                if 0 <= k < taps:
                    term = weight(k, v)
                    accs[ph] = term if accs[ph] is None else accs[ph] + term
        return accs

    def store_phases(g, row0, accs):
        for ph, acc in enumerate(accs):
            sbuf[g, pl.ds(row0 + ph, quarter, stride=ROW_PHASES), :] = acc

    def pool_group(g, kw):
        cols = slice(g * POOL_GROUP, (g + 1) * POOL_GROUP)
        fill(pbuf, g, xp_p[0, :, cols], xp_c[0, :, cols], xp_n[0, :, cols])
        for row0 in range(0, ts, chunk):
            store_phases(g, row0, windowed(pbuf, g, row0, -(kw // 2), kw, lambda k, v: v))
        pos = i * ts + lax.broadcasted_iota(jnp.int32, (ts, 1), 0)
        cnt = (jnp.minimum(pos + kw // 2, seq) - jnp.maximum(pos - kw // 2, 0)).astype(F32)
        pooled = (sbuf[g] / cnt - xp_c[0, :, cols]).astype(BF16)
        mixed = jnp.dot(pooled, pw_ref[0, g], preferred_element_type=F32)
        yb_ref[:, cols] = (mixed * ps_ref[0, :, cols]).astype(yb_ref.dtype)

    for g, kw in enumerate(POOL_SIZES):
        steps.append(functools.partial(pool_group, g, kw))

    def conv_fill(g):
        cols = slice(g * HEAD_DIM, (g + 1) * HEAD_DIM)
        fill(ubuf, g, xc_p[0, :, cols], xc_c[0, :, cols], xc_n[0, :, cols])

    def conv_pass(g, row0):
        cols = slice(g * HEAD_DIM, (g + 1) * HEAD_DIM)
        bias = cb_ref[0, :, cols]
        after = pace()
        if after is not None:
            bias = bias + _zero_after(after)
        accs = windowed(ubuf, g, row0, -(CONV_K // 2), CONV_K,
                        lambda k, v: dw_ref[0, k:k + 1, cols] * v)
        store_phases(g, row0, [acc + bias for acc in accs])

    def conv_norm():
        tot = sum(jnp.sum(sbuf[g], axis=-1, keepdims=True) for g in range(CONV_GROUPS))
        mu = tot / CONV_W
        sq = sum(jnp.sum((sbuf[g] - mu) ** 2, axis=-1, keepdims=True) for g in range(CONV_GROUPS))
        inv = lax.rsqrt(sq / CONV_W + EPS)
        for g in range(CONV_GROUPS):
            cols = slice(g * HEAD_DIM, (g + 1) * HEAD_DIM)
            y = (sbuf[g] - mu) * inv * lg_ref[0, :, cols] + lb_ref[0, :, cols]
            yc_ref[:, cols] = (y * _sigmoid(y)).astype(yc_ref.dtype)

    for g in range(CONV_GROUPS):
        steps.append(functools.partial(conv_fill, g))
        for row0 in range(0, ts, chunk):
            steps.append(functools.partial(conv_pass, g, row0))
    steps.append(conv_norm)
    return steps


def _emit_interleaved(first, second):
    order = sorted([((k + 1) / len(first), 0, k) for k in range(len(first))]
                   + [((k + 0.5) / len(second), 1, k) for k in range(len(second))])
    for _, which, k in order:
        (first, second)[which][k]()


def _merge_body(x_ref, g_ref, o0, o1, o2, l0, l1, l2, xp_p, xp_c, xp_n, xc_p, xc_c, xc_n,
                pw_ref, ps_ref, dw_ref, cb_ref, lg_ref, lb_ref,
                wg_ref, wa_ref, wb_ref, wc_ref, wo_ref, out_ref,
                xn_ref, at_ref, yb_ref, yc_ref, gate_ref, obuf, lbuf, pbuf, ubuf, sbuf, *, seq):
    tm = xn_ref.shape[0]
    d = D_MODEL
    n_g = len(DIL_PAIRS)
    xn_ref[...] = _rms(x_ref[0], g_ref[0]).astype(BF16)

    def attn_head(h):
        for gi, (o, l, (_, r)) in enumerate(zip((o0, o1, o2), (l0, l1, l2), DIL_PAIRS)):
            for c in range(r):
                dst = pl.ds(c, tm // r, stride=r) if r > 1 else pl.ds(0, tm)
                if h == 0:
                    lbuf[gi, dst, :] = l[0, c]
                obuf[gi, h, dst, :] = o[0, c, :, h * HEAD_DIM:(h + 1) * HEAD_DIM].astype(F32)
        ls = [lbuf[gi, :, h:h + 1] for gi in range(n_g)]
        mx = jnp.maximum(jnp.maximum(ls[0], ls[1]), ls[2])
        es = [jnp.exp(t - mx) for t in ls]
        den = es[0] + es[1] + es[2]
        acc = sum((es[gi] / den) * obuf[gi, h] for gi in range(n_g))
        at_ref[:, h * HEAD_DIM:(h + 1) * HEAD_DIM] = acc.astype(BF16)

    last_gate = [None]

    def gate_chunk(bi, c):
        z = jnp.dot(xn_ref[...], wg_ref[0, :, bi * d + c * GATE_CHUNK:bi * d + (c + 1) * GATE_CHUNK],
                    preferred_element_type=F32)
        gate_ref[bi, :, c * GATE_CHUNK:(c + 1) * GATE_CHUNK] = z
        last_gate[0] = z[tm - 1:tm, GATE_CHUNK - HEAD_DIM:GATE_CHUNK]

    vpu_steps = [functools.partial(attn_head, h) for h in range(HEADS_PER_GROUP)]
    vpu_steps += _pool_conv_steps(xp_p, xp_c, xp_n, xc_p, xc_c, xc_n, pw_ref, ps_ref, dw_ref, cb_ref,
                                  lg_ref, lb_ref, yb_ref, yc_ref, pbuf, ubuf, sbuf, ts=tm, seq=seq,
                                  pace=lambda: last_gate[0])
    mxu_steps = [functools.partial(gate_chunk, bi, c) for bi in range(3) for c in range(d // GATE_CHUNK)]
    _emit_interleaved(vpu_steps, mxu_steps)

    merged = None
    for bi, (br, w) in enumerate(((at_ref, wa_ref), (yb_ref, wb_ref), (yc_ref, wc_ref))):
        y = _sigmoid(gate_ref[bi]) * jnp.dot(br[...], w[0], preferred_element_type=F32)
        merged = y if merged is None else merged + y
    out_ref[0] = x_ref[0] + jnp.dot(merged.astype(BF16), wo_ref[0], preferred_element_type=F32)


def _merge(x, norm, os_, lses, xp, xc, w, layer, *, tm=512):
    b, s, d = x.shape
    tm = min(tm, s)
    per = tm // HALO
    nh = s // HALO
    row = lambda wd: pl.BlockSpec((1, tm, wd), lambda i, j: (i, j, 0))
    cls = lambda r, wd: pl.BlockSpec((1, r, tm // r, wd), lambda i, j: (i, 0, j, 0))

    def halo_specs(wd):
        return (pl.BlockSpec((1, HALO, wd), lambda i, j: (i, jnp.maximum(j * per - 1, 0), 0)),
                row(wd),
                pl.BlockSpec((1, HALO, wd), lambda i, j: (i, jnp.minimum((j + 1) * per, nh - 1), 0)))

    rs = [r for _, r in DIL_PAIRS]
    vec = lambda wd: _resident((1, 1, wd), (layer, 0, 0))
    wbr = _resident((1, ATTN_W, d), (layer, 0, 0))
    tile_f32 = lambda groups, rows: pltpu.VMEM((groups, rows, HEAD_DIM), F32)
    return pl.pallas_call(
        functools.partial(_merge_body, seq=s),
        out_shape=jax.ShapeDtypeStruct(x.shape, F32),
        grid=(b, s // tm),
        in_specs=[row(d), vec(d),
                  *[cls(r, ATTN_W) for r in rs], *[cls(r, HEAD_DIM) for r in rs],
                  *halo_specs(POOL_W), *halo_specs(CONV_W),
                  _resident((1, len(POOL_SIZES), POOL_GROUP, POOL_GROUP), (layer, 0, 0, 0)),
                  vec(POOL_W),
                  _resident((1, CONV_K, CONV_W), (layer, 0, 0)),
                  vec(CONV_W), vec(CONV_W), vec(CONV_W),
                  pl.BlockSpec((1, d, 3 * d), lambda i, j: (layer, 0, PROJ_W // (3 * d)),
                               pipeline_mode=pl.Buffered(1)),
                  wbr, wbr, wbr, _resident((1, d, d), (layer, 0, 0))],
        out_specs=row(d),
        scratch_shapes=[pltpu.VMEM((tm, d), BF16),
                        pltpu.VMEM((tm, ATTN_W), BF16), pltpu.VMEM((tm, POOL_W), BF16),
                        pltpu.VMEM((tm, CONV_W), BF16),
                        pltpu.VMEM((3, tm, d), F32),
                        pltpu.VMEM((len(rs), HEADS_PER_GROUP, tm, HEAD_DIM), F32),
                        tile_f32(len(rs), tm),
                        tile_f32(len(POOL_SIZES), tm + 2 * HALO),
                        tile_f32(CONV_GROUPS, tm + 2 * HALO),
                        tile_f32(CONV_GROUPS, tm)],
        compiler_params=_params(2),
        name="merge_out",
    )(x, norm, *os_, *lses, xp, xp, xp, xc, xc, xc,
      w["pool_w"], w["pool_scale"], w["conv_dw"], w["conv_b"], w["conv_ln_g"], w["conv_ln_b"],
      w["w_in"], w["w_br_attn"], w["w_br_pool"], w["w_br_conv"], w["w_out"])


def _memkv_body(m_ref, g_ref, w_ref, k_ref, v_ref):
    mn = _rms(m_ref[0], g_ref[0]).astype(BF16)
    kv = jnp.dot(mn, w_ref[0], preferred_element_type=F32)
    k_ref[0] = kv[:, :X_W].astype(BF16)
    v_ref[0] = kv[:, X_W:].astype(BF16)


def _memkv(mem, norm, wkv, layer):
    b, m, d = mem.shape
    out = pl.BlockSpec((1, m, X_W), lambda i: (i, 0, 0))
    return pl.pallas_call(
        _memkv_body,
        out_shape=(jax.ShapeDtypeStruct((b, m, X_W), BF16),) * 2,
        grid=(b,),
        in_specs=[pl.BlockSpec((1, m, d), lambda i: (i, 0, 0)),
                  _resident((1, 1, d), (layer, 0, 0)),
                  _resident((1, d, 2 * X_W), (layer, 0, 0))],
        out_specs=(out, out),
        compiler_params=_params(1),
        name="mem_kv",
    )(mem, norm, wkv)


def _xattn_body(x_ref, g_ref, wq_ref, k_ref, v_ref, wo_ref, out_ref, q_scr, o_scr):
    xf = x_ref[0]
    xn = _rms(xf, g_ref[0]).astype(BF16)
    q_scr[...] = jnp.dot(xn, wq_ref[0], preferred_element_type=F32).astype(BF16)
    scale2 = HEAD_DIM ** -0.5 * LOG2_E
    for h in range(X_HEADS):
        cols = slice(h * HEAD_DIM, (h + 1) * HEAD_DIM)
        s = lax.dot_general(q_scr[:, cols], k_ref[0, :, cols], (((1,), (1,)), ((), ())),
                            preferred_element_type=F32) * scale2
        p = jnp.exp2(s - jnp.max(s, axis=-1, keepdims=True))
        l = jnp.sum(p, axis=-1, keepdims=True)
        o = jnp.dot(p.astype(BF16), v_ref[0, :, cols], preferred_element_type=F32) / l
        o_scr[:, cols] = o.astype(BF16)
    out_ref[0] = xf + jnp.dot(o_scr[...], wo_ref[0], preferred_element_type=F32)


def _xattn(x, norm, wq, kmem, vmem, wo, layer, *, tm=1024):
    b, s, d = x.shape
    m = kmem.shape[1]
    tm = min(tm, s)
    row = pl.BlockSpec((1, tm, d), lambda i, j: (i, j, 0))
    mem = pl.BlockSpec((1, m, X_W), lambda i, j: (i, 0, 0))
    return pl.pallas_call(
        _xattn_body,
        out_shape=jax.ShapeDtypeStruct(x.shape, F32),
        grid=(b, s // tm),
        in_specs=[row, _resident((1, 1, d), (layer, 0, 0)), _resident((1, d, X_W), (layer, 0, 0)),
                  mem, mem, _resident((1, X_W, d), (layer, 0, 0))],
        out_specs=row,
        scratch_shapes=[pltpu.VMEM((tm, X_W), BF16), pltpu.VMEM((tm, X_W), BF16)],
        compiler_params=_params(2),
        name="mem_xattn",
    )(x, norm, wq, kmem, vmem, wo)


def _trunk(x, mem, w, biases, depth):
    for l in range(depth):
        x = _ffn(x, w["ffn1_norm"], w["ffn1_w_gu"], w["ffn1_w_down"], w["final_norm"], l, final=False)
        *qkvs, xp, xc = _proj(x, w["mix_norm"], w["w_in"], l)
        outs = [_attn_group(qkv, bias) for qkv, bias in zip(qkvs, biases)]
        x = _merge(x, w["mix_norm"], [o for o, _ in outs], [s for _, s in outs], xp, xc, w, l)
        kmem, vmem = _memkv(mem, w["mem_norm"], w["xattn_wkv"], l)
        x = _xattn(x, w["xattn_norm"], w["xattn_wq"], kmem, vmem, w["xattn_wo"], l)
        x = _ffn(x, w["ffn2_norm"], w["ffn2_w_gu"], w["ffn2_w_down"], w["final_norm"], l,
                 final=(l == depth - 1))
    return x


def kernel(x_prompt, x_sample, mem_prompt, mem_sample, rel_bias, ffn1_norm, ffn1_w_gu, ffn1_w_down, mix_norm, w_in, pool_w, pool_scale, conv_dw, conv_b, conv_ln_g, conv_ln_b, w_br_attn, w_br_pool, w_br_conv, w_out, xattn_norm, mem_norm, xattn_wq, xattn_wkv, xattn_wo, ffn2_norm, ffn2_w_gu, ffn2_w_down, final_norm):
    depth = w_in.shape[0]
    mat = lambda t: t.astype(BF16)
    vec = lambda t: t.reshape(depth, 1, t.shape[-1])
    w = dict(
        ffn1_norm=vec(ffn1_norm), ffn1_w_gu=mat(ffn1_w_gu), ffn1_w_down=mat(ffn1_w_down),
        mix_norm=vec(mix_norm), w_in=mat(w_in), pool_w=mat(pool_w), pool_scale=vec(pool_scale),
        conv_dw=conv_dw, conv_b=vec(conv_b), conv_ln_g=vec(conv_ln_g), conv_ln_b=vec(conv_ln_b),
        w_br_attn=mat(w_br_attn), w_br_pool=mat(w_br_pool), w_br_conv=mat(w_br_conv), w_out=mat(w_out),
        xattn_norm=vec(xattn_norm), mem_norm=vec(mem_norm), xattn_wq=mat(xattn_wq),
        xattn_wkv=mat(xattn_wkv), xattn_wo=mat(xattn_wo),
        ffn2_norm=vec(ffn2_norm), ffn2_w_gu=mat(ffn2_w_gu), ffn2_w_down=mat(ffn2_w_down),
        final_norm=final_norm.reshape(1, -1),
    )
    biases = [_band_bias(rel_bias, g, r) for g, (_, r) in enumerate(DIL_PAIRS)]
    y_prompt = _trunk(x_prompt, mem_prompt, w, biases, depth)
    y_sample = _trunk(x_sample, mem_sample, w, biases, depth)
    return (y_prompt, y_sample)
```

```python
import functools
import math

import jax
import jax.numpy as jnp
from jax import lax
from jax.experimental import pallas as pl
from jax.experimental.pallas import tpu as pltpu

F32 = jnp.float32
BF16 = jnp.bfloat16

D_MODEL = 1024
D_FF = 2816
DIL_PAIRS = ((128, 1), (512, 4), (2048, 16))
HEADS_PER_GROUP = 4
N_HEADS_A = len(DIL_PAIRS) * HEADS_PER_GROUP
HEAD_DIM = 128
QKV_W = N_HEADS_A * HEAD_DIM
ATTN_W = HEADS_PER_GROUP * HEAD_DIM
BAND_HALF = 64
REL_BUCKETS = 32
REL_MAX_EXACT = 8
REL_MAX_DIST = 1024
POOL_SIZES = (2, 4, 8, 16)
POOL_GROUP = 128
POOL_W = len(POOL_SIZES) * POOL_GROUP
CONV_W = 512
CONV_K = 31
CONV_GROUPS = CONV_W // HEAD_DIM
X_HEADS = 4
X_W = X_HEADS * HEAD_DIM
PROJ_W = 3 * QKV_W + POOL_W + 2 * CONV_W
EPS = 1e-6
NEG_INF = -1e30
LOG2_E = math.log2(math.e)
LN_2 = math.log(2.0)

HALO = 16
CONV_CHUNK = 256
ROW_PHASES = 4
GATE_CHUNK = 256
Q_SUB = 128
ATTN_TILE = 2048
VMEM_LIMIT = 56 * 1024 * 1024


def _params(n_axes):
    return pltpu.CompilerParams(dimension_semantics=("arbitrary",) * n_axes,
                                vmem_limit_bytes=VMEM_LIMIT)


def _resident(shape, index):
    return pl.BlockSpec(shape, lambda *_: index, pipeline_mode=pl.Buffered(1))


def _rms(xf, g):
    ms = jnp.mean(xf * xf, axis=-1, keepdims=True)
    return xf * lax.rsqrt(ms + EPS) * g


def _sigmoid(x):
    return 1.0 / (1.0 + jnp.exp(-x))


def _zero_after(v):
    bits = pltpu.bitcast(v, jnp.uint32)
    return pltpu.bitcast((bits >> 16) >> 16, F32)


def _ffn_body(x_ref, g_ref, wgu_ref, wd_ref, fg_ref, o_ref, xn_ref, act_ref, *, ck, final):
    xf = x_ref[0]
    xn_ref[...] = _rms(xf, g_ref[0]).astype(BF16)
    for c in range(D_FF // ck):
        xn = xn_ref[...]
        a = jnp.dot(xn, wgu_ref[0, :, c * ck:(c + 1) * ck], preferred_element_type=F32)
        u = jnp.dot(xn, wgu_ref[0, :, D_FF + c * ck:D_FF + (c + 1) * ck], preferred_element_type=F32)
        act_ref[:, c * ck:(c + 1) * ck] = (a * _sigmoid(a) * u).astype(BF16)
    y = xf + 0.5 * jnp.dot(act_ref[...], wd_ref[0], preferred_element_type=F32)
    if final:
        y = _rms(y, fg_ref[...])
    o_ref[0] = y


def _ffn(x, norm, w_gu, w_down, final_g, layer, *, final, tm=1024, ck=256):
    b, s, d = x.shape
    tm = min(tm, s)
    body = functools.partial(_ffn_body, ck=ck, final=final)
    return pl.pallas_call(
        body,
        out_shape=jax.ShapeDtypeStruct(x.shape, F32),
        grid=(b, s // tm),
        in_specs=[
            pl.BlockSpec((1, tm, d), lambda i, j: (i, j, 0)),
            _resident((1, 1, d), (layer, 0, 0)),
            _resident((1, d, 2 * D_FF), (layer, 0, 0)),
            _resident((1, D_FF, d), (layer, 0, 0)),
            _resident((1, d), (0, 0)),
        ],
        out_specs=pl.BlockSpec((1, tm, d), lambda i, j: (i, j, 0)),
        scratch_shapes=[pltpu.VMEM((tm, d), BF16), pltpu.VMEM((tm, D_FF), BF16)],
        compiler_params=_params(2),
        name="ffn",
    )(x, norm, w_gu, w_down, final_g)


def _proj_body(x_ref, g_ref, w_ref, qkv0_ref, qkv1_ref, qkv2_ref, xp_ref, xc_ref, xno_ref,
               xn_ref, xn4_ref, xn16_ref, xs_ref):
    tm = xn_ref.shape[0]
    xn = _rms(x_ref[0], g_ref[0])
    xn_ref[...] = xn.astype(BF16)
    xno_ref[0] = xn_ref[...]

    def chunk(lhs_ref, base):
        return jnp.dot(lhs_ref[...], w_ref[0, :, base:base + ATTN_W], preferred_element_type=F32)

    for part in range(3):
        qkv0_ref[0, 0, :, part * ATTN_W:(part + 1) * ATTN_W] = chunk(xn_ref, part * QKV_W).astype(BF16)
    xp_ref[0] = chunk(xn_ref, 3 * QKV_W)
    xc_ref[0] = chunk(xn_ref, 3 * QKV_W + POOL_W) * _sigmoid(chunk(xn_ref, 3 * QKV_W + POOL_W + CONV_W))

    for t in range(xs_ref.shape[0]):
        xs_ref[t] = xn[:, t * HEAD_DIM:(t + 1) * HEAD_DIM]
    for g, (lhs_ref, out_ref) in ((1, (xn4_ref, qkv1_ref)), (2, (xn16_ref, qkv2_ref))):
        r = DIL_PAIRS[g][1]
        n = tm // r
        for c in range(r):
            for t in range(xs_ref.shape[0]):
                lhs_ref[c * n:(c + 1) * n, t * HEAD_DIM:(t + 1) * HEAD_DIM] = (
                    xs_ref[t, pl.ds(c, n, stride=r), :].astype(BF16))
        for part in range(3):
            y = chunk(lhs_ref, part * QKV_W + g * ATTN_W).astype(BF16)
            for c in range(r):
                out_ref[0, c, :, part * ATTN_W:(part + 1) * ATTN_W] = y[c * n:(c + 1) * n]


def _proj(x, norm, w_in, layer, *, tm=512):
    b, s, d = x.shape
    tm = min(tm, s)
    row = lambda w: pl.BlockSpec((1, tm, w), lambda i, j: (i, j, 0))
    rs = [r for _, r in DIL_PAIRS]
    qkv_shape = lambda r: jax.ShapeDtypeStruct((b, r, s // r, 3 * ATTN_W), BF16)
    qkv_spec = lambda r: pl.BlockSpec((1, r, tm // r, 3 * ATTN_W), lambda i, j: (i, 0, j, 0))
    return pl.pallas_call(
        _proj_body,
        out_shape=(*[qkv_shape(r) for r in rs],
                   jax.ShapeDtypeStruct((b, s, POOL_W), F32),
                   jax.ShapeDtypeStruct((b, s, CONV_W), F32),
                   jax.ShapeDtypeStruct((b, s, d), BF16)),
        grid=(b, s // tm),
        in_specs=[row(d), _resident((1, 1, d), (layer, 0, 0)),
                  _resident((1, d, PROJ_W), (layer, 0, 0))],
        out_specs=(*[qkv_spec(r) for r in rs], row(POOL_W), row(CONV_W), row(d)),
        scratch_shapes=[pltpu.VMEM((tm, d), BF16), pltpu.VMEM((tm, d), BF16), pltpu.VMEM((tm, d), BF16),
                        pltpu.VMEM((d // HEAD_DIM, tm, HEAD_DIM), F32)],
        compiler_params=_params(2),
        name="in_proj",
    )(x, norm, w_in)


def _attn_body(q_ref, kp_ref, kc_ref, kn_ref, vp_ref, vc_ref, vn_ref, bias_ref,
               o_ref, lse_ref, kedge, vedge, *, tl, seq_l):
    h64 = BAND_HALF
    n_sub = tl // Q_SUB
    win = 2 * Q_SUB
    row0 = pl.program_id(2) * tl
    scale2 = HEAD_DIM ** -0.5 * LOG2_E
    lane = lax.broadcasted_iota(jnp.int32, (Q_SUB, HEAD_DIM), 1)
    for cc in range(q_ref.shape[0]):
        for edge, prev, cur, nxt in ((kedge, kp_ref, kc_ref, kn_ref), (vedge, vp_ref, vc_ref, vn_ref)):
            if n_sub == 1:
                edge[cc, 0, 0:h64] = prev[cc]
                edge[cc, 0, h64:h64 + tl] = cur[cc]
                edge[cc, 0, h64 + tl:win] = nxt[cc]
            else:
                edge[cc, 0, 0:h64] = prev[cc]
                edge[cc, 0, h64:win] = cur[cc, 0:win - h64]
                edge[cc, 1, 0:win - h64] = cur[cc, tl - (win - h64):tl]
                edge[cc, 1, win - h64:win] = nxt[cc]

        def window(edge, cur, j, cols):
            if j == 0:
                return edge[cc, 0, :, cols]
            if j == n_sub - 1:
                return edge[cc, 1, :, cols]
            return cur[cc, j * Q_SUB - h64:j * Q_SUB - h64 + win, cols]

        for j in range(n_sub):
            rows = slice(j * Q_SUB, (j + 1) * Q_SUB)
            kmask = None
            if j == 0 or j == n_sub - 1:
                kpos = row0 + (j * Q_SUB - h64) + lax.broadcasted_iota(jnp.int32, (1, win), 1)
                kmask = jnp.where((kpos >= 0) & (kpos < seq_l), 0.0, NEG_INF).astype(F32)
            lse_blk = jnp.zeros((Q_SUB, HEAD_DIM), F32)
            for h in range(HEADS_PER_GROUP):
                cols = slice(h * HEAD_DIM, (h + 1) * HEAD_DIM)
                q = q_ref[cc, rows, cols]
                k = window(kedge, kc_ref, j, cols)
                v = window(vedge, vc_ref, j, cols)
                s = lax.dot_general(q, k, (((1,), (1,)), ((), ())), preferred_element_type=F32)
                s = s * scale2 + bias_ref[h]
                if kmask is not None:
                    s = s + kmask
                mx = jnp.max(s, axis=-1, keepdims=True)
                p = jnp.exp2(s - mx)
                l = jnp.sum(p, axis=-1, keepdims=True)
                o = jnp.dot(p.astype(BF16), v, preferred_element_type=F32) / l
                o_ref[cc, rows, cols] = o.astype(o_ref.dtype)
                lse_blk = jnp.where(lane == h, (mx + jnp.log2(l)) * LN_2, lse_blk)
            lse_ref[cc, rows, :] = lse_blk


def _attn_group(qkv, bias):
    b, r, sl, _ = qkv.shape
    tl = min(ATTN_TILE, sl)
    ncls = min(r, max(1, ATTN_TILE // sl))
    nh = sl // BAND_HALF
    per = tl // BAND_HALF
    cur = lambda part: pl.BlockSpec((None, ncls, tl, ATTN_W), lambda i, c, m: (i, c, m, part))
    prev = lambda part: pl.BlockSpec((None, ncls, BAND_HALF, ATTN_W),
                                     lambda i, c, m: (i, c, jnp.maximum(m * per - 1, 0), part))
    nxt = lambda part: pl.BlockSpec((None, ncls, BAND_HALF, ATTN_W),
                                    lambda i, c, m: (i, c, jnp.minimum((m + 1) * per, nh - 1), part))
    edge = pltpu.VMEM((ncls, 2, 2 * Q_SUB, ATTN_W), BF16)
    return pl.pallas_call(
        functools.partial(_attn_body, tl=tl, seq_l=sl),
        out_shape=(jax.ShapeDtypeStruct((b, r, sl, ATTN_W), BF16),
                   jax.ShapeDtypeStruct((b, r, sl, HEAD_DIM), F32)),
        grid=(b, r // ncls, sl // tl),
        in_specs=[cur(0), prev(1), cur(1), nxt(1), prev(2), cur(2), nxt(2),
                  _resident((HEADS_PER_GROUP, Q_SUB, 2 * Q_SUB), (0, 0, 0))],
        out_specs=(pl.BlockSpec((None, ncls, tl, ATTN_W), lambda i, c, m: (i, c, m, 0)),
                   pl.BlockSpec((None, ncls, tl, HEAD_DIM), lambda i, c, m: (i, c, m, 0))),
        scratch_shapes=[edge, edge],
        compiler_params=_params(3),
        name=f"dilated_attn_r{r}",
    )(qkv, qkv, qkv, qkv, qkv, qkv, qkv, bias)


def _t5_bucket(rel):
    half = REL_BUCKETS // 2
    n = jnp.abs(rel)
    nf = jnp.maximum(n, 1).astype(F32)
    large = REL_MAX_EXACT + (jnp.log(nf / REL_MAX_EXACT) / math.log(REL_MAX_DIST / REL_MAX_EXACT)
                             * (half - REL_MAX_EXACT)).astype(jnp.int32)
    large = jnp.minimum(large, half - 1)
    return jnp.where(rel > 0, half, 0) + jnp.where(n < REL_MAX_EXACT, n, large)


def _band_bias(rel_bias, g, r):
    delta = jnp.arange(2 * Q_SUB)[None, :] - BAND_HALF - jnp.arange(Q_SUB)[:, None]
    bucket = _t5_bucket(r * delta)
    table = rel_bias[:, g * HEADS_PER_GROUP:(g + 1) * HEADS_PER_GROUP].astype(F32)
    onehot = bucket[None, :, :, None] == jnp.arange(REL_BUCKETS)
    tbl = jnp.sum(jnp.where(onehot, table.T[:, None, None, :], 0.0), axis=-1)
    return jnp.where((jnp.abs(delta) <= BAND_HALF)[None], tbl * LOG2_E, NEG_INF)


def _pool_conv_steps(xp_p, xp_c, xp_n, xc_p, xc_c, xc_n, pw_ref, ps_ref, dw_ref, cb_ref, lg_ref, lb_ref,
                     yb_ref, yc_ref, pbuf, ubuf, sbuf, *, ts, seq, pace):
    i = pl.program_id(1)
    has_prev = (i > 0).astype(F32)
    has_next = (i < pl.num_programs(1) - 1).astype(F32)
    chunk = min(CONV_CHUNK, ts)
    quarter = chunk // ROW_PHASES
    steps = []

    def fill(buf, g, prev, cur, nxt):
        buf[g, 0:HALO] = prev * has_prev
        buf[g, HALO:HALO + ts] = cur
        buf[g, HALO + ts:HALO + ts + HALO] = nxt * has_next

    def rows_from(buf, g, start):
        return buf[g, pl.ds(HALO + start, quarter, stride=ROW_PHASES), :]

    def windowed(buf, g, row0, first, taps, weight):
        accs = [None] * ROW_PHASES
        for m in range(taps + ROW_PHASES - 1):
            v = rows_from(buf, g, row0 + first + m)
            for ph in range(ROW_PHASES):
                k = m - ph
                if 0 <= k < taps:
                    term = weight(k, v)
                    accs[ph] = term if accs[ph] is None else accs[ph] + term
        return accs

    def store_phases(g, row0, accs):
        for ph, acc in enumerate(accs):
            sbuf[g, pl.ds(row0 + ph, quarter, stride=ROW_PHASES), :] = acc

    def pool_group(g, kw):
        cols = slice(g * POOL_GROUP, (g + 1) * POOL_GROUP)
        fill(pbuf, g, xp_p[0, :, cols], xp_c[0, :, cols], xp_n[0, :, cols])
        for row0 in range(0, ts, chunk):
            store_phases(g, row0, windowed(pbuf, g, row0, -(kw // 2), kw, lambda k, v: v))
        pos = i * ts + lax.broadcasted_iota(jnp.int32, (ts, 1), 0)
        cnt = (jnp.minimum(pos + kw // 2, seq) - jnp.maximum(pos - kw // 2, 0)).astype(F32)
        pooled = (sbuf[g] / cnt - xp_c[0, :, cols]).astype(BF16)
        mixed = jnp.dot(pooled, pw_ref[0, g], preferred_element_type=F32)
        yb_ref[:, cols] = (mixed * ps_ref[0, :, cols]).astype(yb_ref.dtype)

    for g, kw in enumerate(POOL_SIZES):
        steps.append(functools.partial(pool_group, g, kw))

    def conv_fill(g):
        cols = slice(g * HEAD_DIM, (g + 1) * HEAD_DIM)
        fill(ubuf, g, xc_p[0, :, cols], xc_c[0, :, cols], xc_n[0, :, cols])

    def conv_pass(g, row0):
        cols = slice(g * HEAD_DIM, (g + 1) * HEAD_DIM)
        bias = cb_ref[0, :, cols]
        after = pace()
        if after is not None:
            bias = bias + _zero_after(after)
        accs = windowed(ubuf, g, row0, -(CONV_K // 2), CONV_K,
                        lambda k, v: dw_ref[0, k:k + 1, cols] * v)
        store_phases(g, row0, [acc + bias for acc in accs])

    def conv_norm():
        tot = sum(jnp.sum(sbuf[g], axis=-1, keepdims=True) for g in range(CONV_GROUPS))
        mu = tot / CONV_W
        sq = sum(jnp.sum((sbuf[g] - mu) ** 2, axis=-1, keepdims=True) for g in range(CONV_GROUPS))
        inv = lax.rsqrt(sq / CONV_W + EPS)
        for g in range(CONV_GROUPS):
            cols = slice(g * HEAD_DIM, (g + 1) * HEAD_DIM)
            y = (sbuf[g] - mu) * inv * lg_ref[0, :, cols] + lb_ref[0, :, cols]
            yc_ref[:, cols] = (y * _sigmoid(y)).astype(yc_ref.dtype)

    for g in range(CONV_GROUPS):
        steps.append(functools.partial(conv_fill, g))
        for row0 in range(0, ts, chunk):
            steps.append(functools.partial(conv_pass, g, row0))
    steps.append(conv_norm)
    return steps


def _emit_interleaved(first, second):
    order = sorted([((k + 1) / len(first), 0, k) for k in range(len(first))]
                   + [((k + 0.5) / len(second), 1, k) for k in range(len(second))])
    for _, which, k in order:
        (first, second)[which][k]()


def _merge_body(x_ref, xn_ref, o0, o1, o2, l0, l1, l2, xp_p, xp_c, xp_n, xc_p, xc_c, xc_n,
                pw_ref, ps_ref, dw_ref, cb_ref, lg_ref, lb_ref,
                wg_ref, wa_ref, wb_ref, wc_ref, wo_ref, out_ref,
                at_ref, yb_ref, yc_ref, gate_ref, obuf, lbuf, pbuf, ubuf, sbuf, *, seq):
    tm = at_ref.shape[0]
    d = D_MODEL
    n_g = len(DIL_PAIRS)

    def attn_head(h):
        for gi, (o, l, (_, r)) in enumerate(zip((o0, o1, o2), (l0, l1, l2), DIL_PAIRS)):
            for c in range(r):
                dst = pl.ds(c, tm // r, stride=r) if r > 1 else pl.ds(0, tm)
                if h == 0:
                    lbuf[gi, dst, :] = l[0, c]
                obuf[gi, h, dst, :] = o[0, c, :, h * HEAD_DIM:(h + 1) * HEAD_DIM].astype(F32)
        ls = [lbuf[gi, :, h:h + 1] for gi in range(n_g)]
        mx = jnp.maximum(jnp.maximum(ls[0], ls[1]), ls[2])
        es = [jnp.exp(t - mx) for t in ls]
        den = es[0] + es[1] + es[2]
        acc = sum((es[gi] / den) * obuf[gi, h] for gi in range(n_g))
        at_ref[:, h * HEAD_DIM:(h + 1) * HEAD_DIM] = acc.astype(BF16)

    last_gate = [None]

    def gate_chunk(bi, c):
        z = jnp.dot(xn_ref[0], wg_ref[0, :, bi * d + c * GATE_CHUNK:bi * d + (c + 1) * GATE_CHUNK],
                    preferred_element_type=F32)
        gate_ref[bi, :, c * GATE_CHUNK:(c + 1) * GATE_CHUNK] = z
        last_gate[0] = z[tm - 1:tm, GATE_CHUNK - HEAD_DIM:GATE_CHUNK]

    vpu_steps = [functools.partial(attn_head, h) for h in range(HEADS_PER_GROUP)]
    vpu_steps += _pool_conv_steps(xp_p, xp_c, xp_n, xc_p, xc_c, xc_n, pw_ref, ps_ref, dw_ref, cb_ref,
                                  lg_ref, lb_ref, yb_ref, yc_ref, pbuf, ubuf, sbuf, ts=tm, seq=seq,
                                  pace=lambda: last_gate[0])
    mxu_steps = [functools.partial(gate_chunk, bi, c) for bi in range(3) for c in range(d // GATE_CHUNK)]
    _emit_interleaved(vpu_steps, mxu_steps)

    merged = None
    for bi, (br, w) in enumerate(((at_ref, wa_ref), (yb_ref, wb_ref), (yc_ref, wc_ref))):
        y = _sigmoid(gate_ref[bi]) * jnp.dot(br[...], w[0], preferred_element_type=F32)
        merged = y if merged is None else merged + y
    out_ref[0] = x_ref[0] + jnp.dot(merged.astype(BF16), wo_ref[0], preferred_element_type=F32)


def _merge(x, xn, os_, lses, xp, xc, w, layer, *, tm=512):
    b, s, d = x.shape
    tm = min(tm, s)
    per = tm // HALO
    nh = s // HALO
    row = lambda wd: pl.BlockSpec((1, tm, wd), lambda i, j: (i, j, 0))
    cls = lambda r, wd: pl.BlockSpec((1, r, tm // r, wd), lambda i, j: (i, 0, j, 0))

    def halo_specs(wd):
        return (pl.BlockSpec((1, HALO, wd), lambda i, j: (i, jnp.maximum(j * per - 1, 0), 0)),
                row(wd),
                pl.BlockSpec((1, HALO, wd), lambda i, j: (i, jnp.minimum((j + 1) * per, nh - 1), 0)))

    rs = [r for _, r in DIL_PAIRS]
    vec = lambda wd: _resident((1, 1, wd), (layer, 0, 0))
    wbr = _resident((1, ATTN_W, d), (layer, 0, 0))
    tile_f32 = lambda groups, rows: pltpu.VMEM((groups, rows, HEAD_DIM), F32)
    return pl.pallas_call(
        functools.partial(_merge_body, seq=s),
        out_shape=jax.ShapeDtypeStruct(x.shape, F32),
        grid=(b, s // tm),
        in_specs=[row(d), row(d),
                  *[cls(r, ATTN_W) for r in rs], *[cls(r, HEAD_DIM) for r in rs],
                  *halo_specs(POOL_W), *halo_specs(CONV_W),
                  _resident((1, len(POOL_SIZES), POOL_GROUP, POOL_GROUP), (layer, 0, 0, 0)),
                  vec(POOL_W),
                  _resident((1, CONV_K, CONV_W), (layer, 0, 0)),
                  vec(CONV_W), vec(CONV_W), vec(CONV_W),
                  pl.BlockSpec((1, d, 3 * d), lambda i, j: (layer, 0, PROJ_W // (3 * d)),
                               pipeline_mode=pl.Buffered(1)),
                  wbr, wbr, wbr, _resident((1, d, d), (layer, 0, 0))],
        out_specs=row(d),
        scratch_shapes=[pltpu.VMEM((tm, ATTN_W), BF16), pltpu.VMEM((tm, POOL_W), BF16),
                        pltpu.VMEM((tm, CONV_W), BF16),
                        pltpu.VMEM((3, tm, d), F32),
                        pltpu.VMEM((len(rs), HEADS_PER_GROUP, tm, HEAD_DIM), F32),
                        tile_f32(len(rs), tm),
                        tile_f32(len(POOL_SIZES), tm + 2 * HALO),
                        tile_f32(CONV_GROUPS, tm + 2 * HALO),
                        tile_f32(CONV_GROUPS, tm)],
        compiler_params=_params(2),
        name="merge_out",
    )(x, xn, *os_, *lses, xp, xp, xp, xc, xc, xc,
      w["pool_w"], w["pool_scale"], w["conv_dw"], w["conv_b"], w["conv_ln_g"], w["conv_ln_b"],
      w["w_in"], w["w_br_attn"], w["w_br_pool"], w["w_br_conv"], w["w_out"])


def _memkv_body(m_ref, g_ref, w_ref, k_ref, v_ref):
    mn = _rms(m_ref[0], g_ref[0]).astype(BF16)
    kv = jnp.dot(mn, w_ref[0], preferred_element_type=F32)
    k_ref[0] = kv[:, :X_W].astype(BF16)
    v_ref[0] = kv[:, X_W:].astype(BF16)


def _memkv(mem, norm, wkv, layer):
    b, m, d = mem.shape
    out = pl.BlockSpec((1, m, X_W), lambda i: (i, 0, 0))
    return pl.pallas_call(
        _memkv_body,
        out_shape=(jax.ShapeDtypeStruct((b, m, X_W), BF16),) * 2,
        grid=(b,),
        in_specs=[pl.BlockSpec((1, m, d), lambda i: (i, 0, 0)),
                  _resident((1, 1, d), (layer, 0, 0)),
                  _resident((1, d, 2 * X_W), (layer, 0, 0))],
        out_specs=(out, out),
        compiler_params=_params(1),
        name="mem_kv",
    )(mem, norm, wkv)


def _xattn_body(x_ref, g_ref, wq_ref, k_ref, v_ref, wo_ref, out_ref, q_scr, o_scr):
    xf = x_ref[0]
    xn = _rms(xf, g_ref[0]).astype(BF16)
    q_scr[...] = jnp.dot(xn, wq_ref[0], preferred_element_type=F32).astype(BF16)
    scale2 = HEAD_DIM ** -0.5 * LOG2_E
    for h in range(X_HEADS):
        cols = slice(h * HEAD_DIM, (h + 1) * HEAD_DIM)
        s = lax.dot_general(q_scr[:, cols], k_ref[0, :, cols], (((1,), (1,)), ((), ())),
                            preferred_element_type=F32) * scale2
        p = jnp.exp2(s - jnp.max(s, axis=-1, keepdims=True))
        l = jnp.sum(p, axis=-1, keepdims=True)
        o = jnp.dot(p.astype(BF16), v_ref[0, :, cols], preferred_element_type=F32) / l
        o_scr[:, cols] = o.astype(BF16)
    out_ref[0] = xf + jnp.dot(o_scr[...], wo_ref[0], preferred_element_type=F32)


def _xattn(x, norm, wq, kmem, vmem, wo, layer, *, tm=2048):
    b, s, d = x.shape
    m = kmem.shape[1]
    tm = min(tm, s)
    row = pl.BlockSpec((1, tm, d), lambda i, j: (i, j, 0))
    mem = pl.BlockSpec((1, m, X_W), lambda i, j: (i, 0, 0))
    return pl.pallas_call(
        _xattn_body,
        out_shape=jax.ShapeDtypeStruct(x.shape, F32),
        grid=(b, s // tm),
        in_specs=[row, _resident((1, 1, d), (layer, 0, 0)), _resident((1, d, X_W), (layer, 0, 0)),
                  mem, mem, _resident((1, X_W, d), (layer, 0, 0))],
        out_specs=row,
        scratch_shapes=[pltpu.VMEM((tm, X_W), BF16), pltpu.VMEM((tm, X_W), BF16)],
        compiler_params=_params(2),
        name="mem_xattn",
    )(x, norm, wq, kmem, vmem, wo)


def _trunk(x, mem, w, biases, depth):
    for l in range(depth):
        x = _ffn(x, w["ffn1_norm"], w["ffn1_w_gu"], w["ffn1_w_down"], w["final_norm"], l, final=False)
        *qkvs, xp, xc, xn = _proj(x, w["mix_norm"], w["w_in"], l)
        outs = [_attn_group(qkv, bias) for qkv, bias in zip(qkvs, biases)]
        x = _merge(x, xn, [o for o, _ in outs], [s for _, s in outs], xp, xc, w, l)
        kmem, vmem = _memkv(mem, w["mem_norm"], w["xattn_wkv"], l)
        x = _xattn(x, w["xattn_norm"], w["xattn_wq"], kmem, vmem, w["xattn_wo"], l)
        x = _ffn(x, w["ffn2_norm"], w["ffn2_w_gu"], w["ffn2_w_down"], w["final_norm"], l,
                 final=(l == depth - 1))
    return x


def kernel(x_prompt, x_sample, mem_prompt, mem_sample, rel_bias, ffn1_norm, ffn1_w_gu, ffn1_w_down, mix_norm, w_in, pool_w, pool_scale, conv_dw, conv_b, conv_ln_g, conv_ln_b, w_br_attn, w_br_pool, w_br_conv, w_out, xattn_norm, mem_norm, xattn_wq, xattn_wkv, xattn_wo, ffn2_norm, ffn2_w_gu, ffn2_w_down, final_norm):
    depth = w_in.shape[0]
    mat = lambda t: t.astype(BF16)
    vec = lambda t: t.reshape(depth, 1, t.shape[-1])
    w = dict(
        ffn1_norm=vec(ffn1_norm), ffn1_w_gu=mat(ffn1_w_gu), ffn1_w_down=mat(ffn1_w_down),
        mix_norm=vec(mix_norm), w_in=mat(w_in), pool_w=mat(pool_w), pool_scale=vec(pool_scale),
        conv_dw=conv_dw, conv_b=vec(conv_b), conv_ln_g=vec(conv_ln_g), conv_ln_b=vec(conv_ln_b),
        w_br_attn=mat(w_br_attn), w_br_pool=mat(w_br_pool), w_br_conv=mat(w_br_conv), w_out=mat(w_out),
        xattn_norm=vec(xattn_norm), mem_norm=vec(mem_norm), xattn_wq=mat(xattn_wq),
        xattn_wkv=mat(xattn_wkv), xattn_wo=mat(xattn_wo),
        ffn2_norm=vec(ffn2_norm), ffn2_w_gu=mat(ffn2_w_gu), ffn2_w_down=mat(ffn2_w_down),
        final_norm=final_norm.reshape(1, -1),
    )
    biases = [_band_bias(rel_bias, g, r) for g, (_, r) in enumerate(DIL_PAIRS)]
    y_prompt = _trunk(x_prompt, mem_prompt, w, biases, depth)
    y_sample = _trunk(x_sample, mem_sample, w, biases, depth)
    return (y_prompt, y_sample)
```

```python
import functools
import math

import jax
import jax.numpy as jnp
from jax import lax
from jax.experimental import pallas as pl
from jax.experimental.pallas import tpu as pltpu

F32 = jnp.float32
BF16 = jnp.bfloat16

D_MODEL = 1024
D_FF = 2816
DIL_PAIRS = ((128, 1), (512, 4), (2048, 16))
HEADS_PER_GROUP = 4
N_HEADS_A = len(DIL_PAIRS) * HEADS_PER_GROUP
HEAD_DIM = 128
QKV_W = N_HEADS_A * HEAD_DIM
ATTN_W = HEADS_PER_GROUP * HEAD_DIM
BAND_HALF = 64
REL_BUCKETS = 32
REL_MAX_EXACT = 8
REL_MAX_DIST = 1024
POOL_SIZES = (2, 4, 8, 16)
POOL_GROUP = 128
POOL_W = len(POOL_SIZES) * POOL_GROUP
CONV_W = 512
CONV_K = 31
CONV_GROUPS = CONV_W // HEAD_DIM
X_HEADS = 4
X_W = X_HEADS * HEAD_DIM
PROJ_W = 3 * QKV_W + POOL_W + 2 * CONV_W
EPS = 1e-6
NEG_INF = -1e30
LOG2_E = math.log2(math.e)
LN_2 = math.log(2.0)

HALO = 16
CONV_CHUNK = 256
ROW_PHASES = 4
GATE_CHUNK = 256
Q_SUB = 128
ATTN_TILE = 2048
VMEM_LIMIT = 56 * 1024 * 1024


def _params(n_axes):
    return pltpu.CompilerParams(dimension_semantics=("arbitrary",) * n_axes,
                                vmem_limit_bytes=VMEM_LIMIT)


def _resident(shape, index):
    return pl.BlockSpec(shape, lambda *_: index, pipeline_mode=pl.Buffered(1))


def _rms(xf, g):
    ms = jnp.mean(xf * xf, axis=-1, keepdims=True)
    return xf * lax.rsqrt(ms + EPS) * g


def _sigmoid(x):
    return 1.0 / (1.0 + jnp.exp(-x))


def _zero_after(v):
    bits = pltpu.bitcast(v, jnp.uint32)
    return pltpu.bitcast((bits >> 16) >> 16, F32)


def _mem_attention(xf, g_ref, wq_ref, k_ref, v_ref, wo_ref, q_scr, o_scr):
    xn = _rms(xf, g_ref[0]).astype(BF16)
    q_scr[...] = jnp.dot(xn, wq_ref[0], preferred_element_type=F32).astype(BF16)
    scale2 = HEAD_DIM ** -0.5 * LOG2_E
    for h in range(X_HEADS):
        cols = slice(h * HEAD_DIM, (h + 1) * HEAD_DIM)
        s = lax.dot_general(q_scr[:, cols], k_ref[0, :, cols], (((1,), (1,)), ((), ())),
                            preferred_element_type=F32) * scale2
        p = jnp.exp2(s - jnp.max(s, axis=-1, keepdims=True))
        l = jnp.sum(p, axis=-1, keepdims=True)
        o = jnp.dot(p.astype(BF16), v_ref[0, :, cols], preferred_element_type=F32) / l
        o_scr[:, cols] = o.astype(BF16)
    return xf + jnp.dot(o_scr[...], wo_ref[0], preferred_element_type=F32)


def _ffn_body(x_ref, g_ref, wgu_ref, wd_ref, fg_ref, *rest, ck, final, with_mem):
    if with_mem:
        xg_ref, wq_ref, k_ref, v_ref, wo_ref, o_ref, xn_ref, act_ref, xmid_ref, q_scr, o_scr = rest
        xmid_ref[...] = _mem_attention(x_ref[0], xg_ref, wq_ref, k_ref, v_ref, wo_ref, q_scr, o_scr)
        residual = lambda: xmid_ref[...]
    else:
        o_ref, xn_ref, act_ref = rest
        residual = lambda: x_ref[0]
    xn_ref[...] = _rms(residual(), g_ref[0]).astype(BF16)
    for c in range(D_FF // ck):
        xn = xn_ref[...]
        a = jnp.dot(xn, wgu_ref[0, :, c * ck:(c + 1) * ck], preferred_element_type=F32)
        u = jnp.dot(xn, wgu_ref[0, :, D_FF + c * ck:D_FF + (c + 1) * ck], preferred_element_type=F32)
        act_ref[:, c * ck:(c + 1) * ck] = (a * _sigmoid(a) * u).astype(BF16)
    y = residual() + 0.5 * jnp.dot(act_ref[...], wd_ref[0], preferred_element_type=F32)
    if final:
        y = _rms(y, fg_ref[...])
    o_ref[0] = y


def _ffn(x, norm, w_gu, w_down, final_g, layer, *, final, mem=None, tm=1024, ck=256):
    b, s, d = x.shape
    tm = min(tm, s)
    body = functools.partial(_ffn_body, ck=ck, final=final, with_mem=mem is not None)
    in_specs = [
        pl.BlockSpec((1, tm, d), lambda i, j: (i, j, 0)),
        _resident((1, 1, d), (layer, 0, 0)),
        _resident((1, d, 2 * D_FF), (layer, 0, 0)),
        _resident((1, D_FF, d), (layer, 0, 0)),
        _resident((1, d), (0, 0)),
    ]
    scratch = [pltpu.VMEM((tm, d), BF16), pltpu.VMEM((tm, D_FF), BF16)]
    operands = [x, norm, w_gu, w_down, final_g]
    if mem is not None:
        xnorm, wq, kmem, vmem, wo = mem
        kv = pl.BlockSpec((1, kmem.shape[1], X_W), lambda i, j: (i, 0, 0))
        in_specs += [_resident((1, 1, d), (layer, 0, 0)), _resident((1, d, X_W), (layer, 0, 0)),
                     kv, kv, _resident((1, X_W, d), (layer, 0, 0))]
        scratch += [pltpu.VMEM((tm, d), F32), pltpu.VMEM((tm, X_W), BF16), pltpu.VMEM((tm, X_W), BF16)]
        operands += [xnorm, wq, kmem, vmem, wo]
    return pl.pallas_call(
        body,
        out_shape=jax.ShapeDtypeStruct(x.shape, F32),
        grid=(b, s // tm),
        in_specs=in_specs,
        out_specs=pl.BlockSpec((1, tm, d), lambda i, j: (i, j, 0)),
        scratch_shapes=scratch,
        compiler_params=_params(2),
        name="ffn_mem" if mem is not None else "ffn",
    )(*operands)


def _proj_body(x_ref, g_ref, w_ref, qkv0_ref, qkv1_ref, qkv2_ref, xp_ref, xc_ref, xno_ref,
               xn_ref, xn4_ref, xn16_ref, xs_ref):
    tm = xn_ref.shape[0]
    xn = _rms(x_ref[0], g_ref[0])
    xn_ref[...] = xn.astype(BF16)
    xno_ref[0] = xn_ref[...]

    def chunk(lhs_ref, base):
        return jnp.dot(lhs_ref[...], w_ref[0, :, base:base + ATTN_W], preferred_element_type=F32)

    for part in range(3):
        qkv0_ref[0, 0, :, part * ATTN_W:(part + 1) * ATTN_W] = chunk(xn_ref, part * QKV_W).astype(BF16)
    xp_ref[0] = chunk(xn_ref, 3 * QKV_W)
    xc_ref[0] = chunk(xn_ref, 3 * QKV_W + POOL_W) * _sigmoid(chunk(xn_ref, 3 * QKV_W + POOL_W + CONV_W))

    for t in range(xs_ref.shape[0]):
        xs_ref[t] = xn[:, t * HEAD_DIM:(t + 1) * HEAD_DIM]
    for g, (lhs_ref, out_ref) in ((1, (xn4_ref, qkv1_ref)), (2, (xn16_ref, qkv2_ref))):
        r = DIL_PAIRS[g][1]
        n = tm // r
        for c in range(r):
            for t in range(xs_ref.shape[0]):
                lhs_ref[c * n:(c + 1) * n, t * HEAD_DIM:(t + 1) * HEAD_DIM] = (
                    xs_ref[t, pl.ds(c, n, stride=r), :].astype(BF16))
        for part in range(3):
            y = chunk(lhs_ref, part * QKV_W + g * ATTN_W).astype(BF16)
            for c in range(r):
                out_ref[0, c, :, part * ATTN_W:(part + 1) * ATTN_W] = y[c * n:(c + 1) * n]


def _proj(x, norm, w_in, layer, *, tm=512):
    b, s, d = x.shape
    tm = min(tm, s)
    row = lambda w: pl.BlockSpec((1, tm, w), lambda i, j: (i, j, 0))
    rs = [r for _, r in DIL_PAIRS]
    qkv_shape = lambda r: jax.ShapeDtypeStruct((b, r, s // r, 3 * ATTN_W), BF16)
    qkv_spec = lambda r: pl.BlockSpec((1, r, tm // r, 3 * ATTN_W), lambda i, j: (i, 0, j, 0))
    return pl.pallas_call(
        _proj_body,
        out_shape=(*[qkv_shape(r) for r in rs],
                   jax.ShapeDtypeStruct((b, s, POOL_W), F32),
                   jax.ShapeDtypeStruct((b, s, CONV_W), F32),
                   jax.ShapeDtypeStruct((b, s, d), BF16)),
        grid=(b, s // tm),
        in_specs=[row(d), _resident((1, 1, d), (layer, 0, 0)),
                  _resident((1, d, PROJ_W), (layer, 0, 0))],
        out_specs=(*[qkv_spec(r) for r in rs], row(POOL_W), row(CONV_W), row(d)),
        scratch_shapes=[pltpu.VMEM((tm, d), BF16), pltpu.VMEM((tm, d), BF16), pltpu.VMEM((tm, d), BF16),
                        pltpu.VMEM((d // HEAD_DIM, tm, HEAD_DIM), F32)],
        compiler_params=_params(2),
        name="in_proj",
    )(x, norm, w_in)


def _attn_body(q_ref, kp_ref, kc_ref, kn_ref, vp_ref, vc_ref, vn_ref, bias_ref,
               o_ref, lse_ref, kedge, vedge, *, tl, seq_l):
    h64 = BAND_HALF
    n_sub = tl // Q_SUB
    win = 2 * Q_SUB
    row0 = pl.program_id(2) * tl
    scale2 = HEAD_DIM ** -0.5 * LOG2_E
    lane = lax.broadcasted_iota(jnp.int32, (Q_SUB, HEAD_DIM), 1)
    for cc in range(q_ref.shape[0]):
        for edge, prev, cur, nxt in ((kedge, kp_ref, kc_ref, kn_ref), (vedge, vp_ref, vc_ref, vn_ref)):
            if n_sub == 1:
                edge[cc, 0, 0:h64] = prev[cc]
                edge[cc, 0, h64:h64 + tl] = cur[cc]
                edge[cc, 0, h64 + tl:win] = nxt[cc]
            else:
                edge[cc, 0, 0:h64] = prev[cc]
                edge[cc, 0, h64:win] = cur[cc, 0:win - h64]
                edge[cc, 1, 0:win - h64] = cur[cc, tl - (win - h64):tl]
                edge[cc, 1, win - h64:win] = nxt[cc]

        def window(edge, cur, j, cols):
            if j == 0:
                return edge[cc, 0, :, cols]
            if j == n_sub - 1:
                return edge[cc, 1, :, cols]
            return cur[cc, j * Q_SUB - h64:j * Q_SUB - h64 + win, cols]

        for j in range(n_sub):
            rows = slice(j * Q_SUB, (j + 1) * Q_SUB)
            kmask = None
            if j == 0 or j == n_sub - 1:
                kpos = row0 + (j * Q_SUB - h64) + lax.broadcasted_iota(jnp.int32, (1, win), 1)
                kmask = jnp.where((kpos >= 0) & (kpos < seq_l), 0.0, NEG_INF).astype(F32)
            lse_blk = jnp.zeros((Q_SUB, HEAD_DIM), F32)
            for h in range(HEADS_PER_GROUP):
                cols = slice(h * HEAD_DIM, (h + 1) * HEAD_DIM)
                q = q_ref[cc, rows, cols]
                k = window(kedge, kc_ref, j, cols)
                v = window(vedge, vc_ref, j, cols)
                s = lax.dot_general(q, k, (((1,), (1,)), ((), ())), preferred_element_type=F32)
                s = s * scale2 + bias_ref[h]
                if kmask is not None:
                    s = s + kmask
                mx = jnp.max(s, axis=-1, keepdims=True)
                p = jnp.exp2(s - mx)
                l = jnp.sum(p, axis=-1, keepdims=True)
                o = jnp.dot(p.astype(BF16), v, preferred_element_type=F32) / l
                o_ref[cc, rows, cols] = o.astype(o_ref.dtype)
                lse_blk = jnp.where(lane == h, (mx + jnp.log2(l)) * LN_2, lse_blk)
            lse_ref[cc, rows, :] = lse_blk


def _attn_group(qkv, bias):
    b, r, sl, _ = qkv.shape
    tl = min(ATTN_TILE, sl)
    ncls = min(r, max(1, ATTN_TILE // sl))
    nh = sl // BAND_HALF
    per = tl // BAND_HALF
    cur = lambda part: pl.BlockSpec((None, ncls, tl, ATTN_W), lambda i, c, m: (i, c, m, part))
    prev = lambda part: pl.BlockSpec((None, ncls, BAND_HALF, ATTN_W),
                                     lambda i, c, m: (i, c, jnp.maximum(m * per - 1, 0), part))
    nxt = lambda part: pl.BlockSpec((None, ncls, BAND_HALF, ATTN_W),
                                    lambda i, c, m: (i, c, jnp.minimum((m + 1) * per, nh - 1), part))
    edge = pltpu.VMEM((ncls, 2, 2 * Q_SUB, ATTN_W), BF16)
    return pl.pallas_call(
        functools.partial(_attn_body, tl=tl, seq_l=sl),
        out_shape=(jax.ShapeDtypeStruct((b, r, sl, ATTN_W), BF16),
                   jax.ShapeDtypeStruct((b, r, sl, HEAD_DIM), F32)),
        grid=(b, r // ncls, sl // tl),
        in_specs=[cur(0), prev(1), cur(1), nxt(1), prev(2), cur(2), nxt(2),
                  _resident((HEADS_PER_GROUP, Q_SUB, 2 * Q_SUB), (0, 0, 0))],
        out_specs=(pl.BlockSpec((None, ncls, tl, ATTN_W), lambda i, c, m: (i, c, m, 0)),
                   pl.BlockSpec((None, ncls, tl, HEAD_DIM), lambda i, c, m: (i, c, m, 0))),
        scratch_shapes=[edge, edge],
        compiler_params=_params(3),
        name=f"dilated_attn_r{r}",
    )(qkv, qkv, qkv, qkv, qkv, qkv, qkv, bias)


def _t5_bucket(rel):
    half = REL_BUCKETS // 2
    n = jnp.abs(rel)
    nf = jnp.maximum(n, 1).astype(F32)
    large = REL_MAX_EXACT + (jnp.log(nf / REL_MAX_EXACT) / math.log(REL_MAX_DIST / REL_MAX_EXACT)
                             * (half - REL_MAX_EXACT)).astype(jnp.int32)
    large = jnp.minimum(large, half - 1)
    return jnp.where(rel > 0, half, 0) + jnp.where(n < REL_MAX_EXACT, n, large)


def _band_bias(rel_bias, g, r):
    delta = jnp.arange(2 * Q_SUB)[None, :] - BAND_HALF - jnp.arange(Q_SUB)[:, None]
    bucket = _t5_bucket(r * delta)
    table = rel_bias[:, g * HEADS_PER_GROUP:(g + 1) * HEADS_PER_GROUP].astype(F32)
    onehot = bucket[None, :, :, None] == jnp.arange(REL_BUCKETS)
    tbl = jnp.sum(jnp.where(onehot, table.T[:, None, None, :], 0.0), axis=-1)
    return jnp.where((jnp.abs(delta) <= BAND_HALF)[None], tbl * LOG2_E, NEG_INF)


def _pool_conv_steps(xp_p, xp_c, xp_n, xc_p, xc_c, xc_n, pw_ref, ps_ref, dw_ref, cb_ref, lg_ref, lb_ref,
                     yb_ref, yc_ref, pbuf, ubuf, sbuf, *, ts, seq, pace):
    i = pl.program_id(1)
    has_prev = (i > 0).astype(F32)
    has_next = (i < pl.num_programs(1) - 1).astype(F32)
    chunk = min(CONV_CHUNK, ts)
    quarter = chunk // ROW_PHASES
    steps = []

    def fill(buf, g, prev, cur, nxt):
        buf[g, 0:HALO] = prev * has_prev
        buf[g, HALO:HALO + ts] = cur
        buf[g, HALO + ts:HALO + ts + HALO] = nxt * has_next

    def rows_from(buf, g, start):
        return buf[g, pl.ds(HALO + start, quarter, stride=ROW_PHASES), :]

    def windowed(buf, g, row0, first, taps, weight):
        accs = [None] * ROW_PHASES
        for m in range(taps + ROW_PHASES - 1):
            v = rows_from(buf, g, row0 + first + m)
            for ph in range(ROW_PHASES):
                k = m - ph
                if 0 <= k < taps:
                    term = weight(k, v)
                    accs[ph] = term if accs[ph] is None else accs[ph] + term
        return accs

    def store_phases(g, row0, accs):
        for ph, acc in enumerate(accs):
            sbuf[g, pl.ds(row0 + ph, quarter, stride=ROW_PHASES), :] = acc

    def pool_group(g, kw):
        cols = slice(g * POOL_GROUP, (g + 1) * POOL_GROUP)
        fill(pbuf, g, xp_p[0, :, cols], xp_c[0, :, cols], xp_n[0, :, cols])
        for row0 in range(0, ts, chunk):
            store_phases(g, row0, windowed(pbuf, g, row0, -(kw // 2), kw, lambda k, v: v))
        pos = i * ts + lax.broadcasted_iota(jnp.int32, (ts, 1), 0)
        cnt = (jnp.minimum(pos + kw // 2, seq) - jnp.maximum(pos - kw // 2, 0)).astype(F32)
        pooled = (sbuf[g] / cnt - xp_c[0, :, cols]).astype(BF16)
        mixed = jnp.dot(pooled, pw_ref[0, g], preferred_element_type=F32)
        yb_ref[:, cols] = (mixed * ps_ref[0, :, cols]).astype(yb_ref.dtype)

    for g, kw in enumerate(POOL_SIZES):
        steps.append(functools.partial(pool_group, g, kw))

    def conv_fill(g):
        cols = slice(g * HEAD_DIM, (g + 1) * HEAD_DIM)
        fill(ubuf, g, xc_p[0, :, cols], xc_c[0, :, cols], xc_n[0, :, cols])

    def conv_pass(g, row0):
        cols = slice(g * HEAD_DIM, (g + 1) * HEAD_DIM)
        bias = cb_ref[0, :, cols]
        after = pace()
        if after is not None:
            bias = bias + _zero_after(after)
        accs = windowed(ubuf, g, row0, -(CONV_K // 2), CONV_K,
                        lambda k, v: dw_ref[0, k:k + 1, cols] * v)
        store_phases(g, row0, [acc + bias for acc in accs])

    def conv_norm():
        tot = sum(jnp.sum(sbuf[g], axis=-1, keepdims=True) for g in range(CONV_GROUPS))
        mu = tot / CONV_W
        sq = sum(jnp.sum((sbuf[g] - mu) ** 2, axis=-1, keepdims=True) for g in range(CONV_GROUPS))
        inv = lax.rsqrt(sq / CONV_W + EPS)
        for g in range(CONV_GROUPS):
            cols = slice(g * HEAD_DIM, (g + 1) * HEAD_DIM)
            y = (sbuf[g] - mu) * inv * lg_ref[0, :, cols] + lb_ref[0, :, cols]
            yc_ref[:, cols] = (y * _sigmoid(y)).astype(yc_ref.dtype)

    for g in range(CONV_GROUPS):
        steps.append(functools.partial(conv_fill, g))
        for row0 in range(0, ts, chunk):
            steps.append(functools.partial(conv_pass, g, row0))
    steps.append(conv_norm)
    return steps


def _emit_interleaved(first, second):
    order = sorted([((k + 1) / len(first), 0, k) for k in range(len(first))]
                   + [((k + 0.5) / len(second), 1, k) for k in range(len(second))])
    for _, which, k in order:
        (first, second)[which][k]()


def _merge_body(x_ref, xn_ref, o0, o1, o2, l0, l1, l2, xp_p, xp_c, xp_n, xc_p, xc_c, xc_n,
                pw_ref, ps_ref, dw_ref, cb_ref, lg_ref, lb_ref,
                wg_ref, wa_ref, wb_ref, wc_ref, wo_ref, out_ref,
                at_ref, yb_ref, yc_ref, gate_ref, obuf, lbuf, pbuf, ubuf, sbuf, *, seq):
    tm = at_ref.shape[0]
    d = D_MODEL
    n_g = len(DIL_PAIRS)

    def attn_head(h):
        for gi, (o, l, (_, r)) in enumerate(zip((o0, o1, o2), (l0, l1, l2), DIL_PAIRS)):
            for c in range(r):
                dst = pl.ds(c, tm // r, stride=r) if r > 1 else pl.ds(0, tm)
                if h == 0:
                    lbuf[gi, dst, :] = l[0, c]
                obuf[gi, h, dst, :] = o[0, c, :, h * HEAD_DIM:(h + 1) * HEAD_DIM].astype(F32)
        ls = [lbuf[gi, :, h:h + 1] for gi in range(n_g)]
        mx = jnp.maximum(jnp.maximum(ls[0], ls[1]), ls[2])
        es = [jnp.exp(t - mx) for t in ls]
        den = es[0] + es[1] + es[2]
        acc = sum((es[gi] / den) * obuf[gi, h] for gi in range(n_g))
        at_ref[:, h * HEAD_DIM:(h + 1) * HEAD_DIM] = acc.astype(BF16)

    last_gate = [None]

    def gate_chunk(bi, c):
        z = jnp.dot(xn_ref[0], wg_ref[0, :, bi * d + c * GATE_CHUNK:bi * d + (c + 1) * GATE_CHUNK],
                    preferred_element_type=F32)
        gate_ref[bi, :, c * GATE_CHUNK:(c + 1) * GATE_CHUNK] = z
        last_gate[0] = z[tm - 1:tm, GATE_CHUNK - HEAD_DIM:GATE_CHUNK]

    vpu_steps = [functools.partial(attn_head, h) for h in range(HEADS_PER_GROUP)]
    vpu_steps += _pool_conv_steps(xp_p, xp_c, xp_n, xc_p, xc_c, xc_n, pw_ref, ps_ref, dw_ref, cb_ref,
                                  lg_ref, lb_ref, yb_ref, yc_ref, pbuf, ubuf, sbuf, ts=tm, seq=seq,
                                  pace=lambda: last_gate[0])
    mxu_steps = [functools.partial(gate_chunk, bi, c) for bi in range(3) for c in range(d // GATE_CHUNK)]
    _emit_interleaved(vpu_steps, mxu_steps)

    merged = None
    for bi, (br, w) in enumerate(((at_ref, wa_ref), (yb_ref, wb_ref), (yc_ref, wc_ref))):
        y = _sigmoid(gate_ref[bi]) * jnp.dot(br[...], w[0], preferred_element_type=F32)
        merged = y if merged is None else merged + y
    out_ref[0] = x_ref[0] + jnp.dot(merged.astype(BF16), wo_ref[0], preferred_element_type=F32)


def _merge(x, xn, os_, lses, xp, xc, w, layer, *, tm=512):
    b, s, d = x.shape
    tm = min(tm, s)
    per = tm // HALO
    nh = s // HALO
    row = lambda wd: pl.BlockSpec((1, tm, wd), lambda i, j: (i, j, 0))
    cls = lambda r, wd: pl.BlockSpec((1, r, tm // r, wd), lambda i, j: (i, 0, j, 0))

    def halo_specs(wd):
        return (pl.BlockSpec((1, HALO, wd), lambda i, j: (i, jnp.maximum(j * per - 1, 0), 0)),
                row(wd),
                pl.BlockSpec((1, HALO, wd), lambda i, j: (i, jnp.minimum((j + 1) * per, nh - 1), 0)))

    rs = [r for _, r in DIL_PAIRS]
    vec = lambda wd: _resident((1, 1, wd), (layer, 0, 0))
    wbr = _resident((1, ATTN_W, d), (layer, 0, 0))
    tile_f32 = lambda groups, rows: pltpu.VMEM((groups, rows, HEAD_DIM), F32)
    return pl.pallas_call(
        functools.partial(_merge_body, seq=s),
        out_shape=jax.ShapeDtypeStruct(x.shape, F32),
        grid=(b, s // tm),
        in_specs=[row(d), row(d),
                  *[cls(r, ATTN_W) for r in rs], *[cls(r, HEAD_DIM) for r in rs],
                  *halo_specs(POOL_W), *halo_specs(CONV_W),
                  _resident((1, len(POOL_SIZES), POOL_GROUP, POOL_GROUP), (layer, 0, 0, 0)),
                  vec(POOL_W),
                  _resident((1, CONV_K, CONV_W), (layer, 0, 0)),
                  vec(CONV_W), vec(CONV_W), vec(CONV_W),
                  pl.BlockSpec((1, d, 3 * d), lambda i, j: (layer, 0, PROJ_W // (3 * d)),
                               pipeline_mode=pl.Buffered(1)),
                  wbr, wbr, wbr, _resident((1, d, d), (layer, 0, 0))],
        out_specs=row(d),
        scratch_shapes=[pltpu.VMEM((tm, ATTN_W), BF16), pltpu.VMEM((tm, POOL_W), BF16),
                        pltpu.VMEM((tm, CONV_W), BF16),
                        pltpu.VMEM((3, tm, d), F32),
                        pltpu.VMEM((len(rs), HEADS_PER_GROUP, tm, HEAD_DIM), F32),
                        tile_f32(len(rs), tm),
                        tile_f32(len(POOL_SIZES), tm + 2 * HALO),
                        tile_f32(CONV_GROUPS, tm + 2 * HALO),
                        tile_f32(CONV_GROUPS, tm)],
        compiler_params=_params(2),
        name="merge_out",
    )(x, xn, *os_, *lses, xp, xp, xp, xc, xc, xc,
      w["pool_w"], w["pool_scale"], w["conv_dw"], w["conv_b"], w["conv_ln_g"], w["conv_ln_b"],
      w["w_in"], w["w_br_attn"], w["w_br_pool"], w["w_br_conv"], w["w_out"])


def _memkv_body(m_ref, g_ref, w_ref, k_ref, v_ref):
    mn = _rms(m_ref[0], g_ref[0]).astype(BF16)
    kv = jnp.dot(mn, w_ref[0], preferred_element_type=F32)
    k_ref[0] = kv[:, :X_W].astype(BF16)
    v_ref[0] = kv[:, X_W:].astype(BF16)


def _memkv(mem, norm, wkv, layer):
    b, m, d = mem.shape
    out = pl.BlockSpec((1, m, X_W), lambda i: (i, 0, 0))
    return pl.pallas_call(
        _memkv_body,
        out_shape=(jax.ShapeDtypeStruct((b, m, X_W), BF16),) * 2,
        grid=(b,),
        in_specs=[pl.BlockSpec((1, m, d), lambda i: (i, 0, 0)),
                  _resident((1, 1, d), (layer, 0, 0)),
                  _resident((1, d, 2 * X_W), (layer, 0, 0))],
        out_specs=(out, out),
        compiler_params=_params(1),
        name="mem_kv",
    )(mem, norm, wkv)


def _trunk(x, mem, w, biases, depth):
    for l in range(depth):
        x = _ffn(x, w["ffn1_norm"], w["ffn1_w_gu"], w["ffn1_w_down"], w["final_norm"], l, final=False)
        *qkvs, xp, xc, xn = _proj(x, w["mix_norm"], w["w_in"], l)
        outs = [_attn_group(qkv, bias) for qkv, bias in zip(qkvs, biases)]
        x = _merge(x, xn, [o for o, _ in outs], [s for _, s in outs], xp, xc, w, l)
        kmem, vmem = _memkv(mem, w["mem_norm"], w["xattn_wkv"], l)
        x = _ffn(x, w["ffn2_norm"], w["ffn2_w_gu"], w["ffn2_w_down"], w["final_norm"], l,
                 final=(l == depth - 1),
                 mem=(w["xattn_norm"], w["xattn_wq"], kmem, vmem, w["xattn_wo"]))
    return x


def kernel(x_prompt, x_sample, mem_prompt, mem_sample, rel_bias, ffn1_norm, ffn1_w_gu, ffn1_w_down, mix_norm, w_in, pool_w, pool_scale, conv_dw, conv_b, conv_ln_g, conv_ln_b, w_br_attn, w_br_pool, w_br_conv, w_out, xattn_norm, mem_norm, xattn_wq, xattn_wkv, xattn_wo, ffn2_norm, ffn2_w_gu, ffn2_w_down, final_norm):
    depth = w_in.shape[0]
    mat = lambda t: t.astype(BF16)
    vec = lambda t: t.reshape(depth, 1, t.shape[-1])
    w = dict(
        ffn1_norm=vec(ffn1_norm), ffn1_w_gu=mat(ffn1_w_gu), ffn1_w_down=mat(ffn1_w_down),
        mix_norm=vec(mix_norm), w_in=mat(w_in), pool_w=mat(pool_w), pool_scale=vec(pool_scale),
        conv_dw=conv_dw, conv_b=vec(conv_b), conv_ln_g=vec(conv_ln_g), conv_ln_b=vec(conv_ln_b),
        w_br_attn=mat(w_br_attn), w_br_pool=mat(w_br_pool), w_br_conv=mat(w_br_conv), w_out=mat(w_out),
        xattn_norm=vec(xattn_norm), mem_norm=vec(mem_norm), xattn_wq=mat(xattn_wq),
        xattn_wkv=mat(xattn_wkv), xattn_wo=mat(xattn_wo),
        ffn2_norm=vec(ffn2_norm), ffn2_w_gu=mat(ffn2_w_gu), ffn2_w_down=mat(ffn2_w_down),
        final_norm=final_norm.reshape(1, -1),
    )
    biases = [_band_bias(rel_bias, g, r) for g, (_, r) in enumerate(DIL_PAIRS)]
    y_prompt = _trunk(x_prompt, mem_prompt, w, biases, depth)
    y_sample = _trunk(x_sample, mem_sample, w, biases, depth)
    return (y_prompt, y_sample)
```

```python
import functools
import math

import jax
import jax.numpy as jnp
from jax import lax
from jax.experimental import pallas as pl
from jax.experimental.pallas import tpu as pltpu

F32 = jnp.float32
BF16 = jnp.bfloat16

D_MODEL = 1024
D_FF = 2816
DIL_PAIRS = ((128, 1), (512, 4), (2048, 16))
HEADS_PER_GROUP = 4
N_HEADS_A = len(DIL_PAIRS) * HEADS_PER_GROUP
HEAD_DIM = 128
QKV_W = N_HEADS_A * HEAD_DIM
ATTN_W = HEADS_PER_GROUP * HEAD_DIM
BAND_HALF = 64
REL_BUCKETS = 32
REL_MAX_EXACT = 8
REL_MAX_DIST = 1024
POOL_SIZES = (2, 4, 8, 16)
POOL_GROUP = 128
POOL_W = len(POOL_SIZES) * POOL_GROUP
CONV_W = 512
CONV_K = 31
CONV_GROUPS = CONV_W // HEAD_DIM
X_HEADS = 4
X_W = X_HEADS * HEAD_DIM
PROJ_W = 3 * QKV_W + POOL_W + 2 * CONV_W
EPS = 1e-6
NEG_INF = -1e30
LOG2_E = math.log2(math.e)
LN_2 = math.log(2.0)

HALO = 16
CONV_CHUNK = 256
ROW_PHASES = 4
Q_SUB = 128
ATTN_TILE = 2048
VMEM_LIMIT = 56 * 1024 * 1024


def _params(n_axes):
    return pltpu.CompilerParams(dimension_semantics=("arbitrary",) * n_axes,
                                vmem_limit_bytes=VMEM_LIMIT)


def _resident(shape, index):
    return pl.BlockSpec(shape, lambda *_: index, pipeline_mode=pl.Buffered(1))


def _rms(xf, g):
    ms = jnp.mean(xf * xf, axis=-1, keepdims=True)
    return xf * lax.rsqrt(ms + EPS) * g


def _sigmoid(x):
    return 1.0 / (1.0 + jnp.exp(-x))


def _zero_after(v):
    bits = pltpu.bitcast(v, jnp.uint32)[0:1]
    return pltpu.bitcast((bits >> 16) >> 16, F32)


def _mem_attention(xf, g_ref, wq_ref, k_ref, v_ref, wo_ref, q_scr, o_scr):
    xn = _rms(xf, g_ref[0]).astype(BF16)
    q_scr[...] = jnp.dot(xn, wq_ref[0], preferred_element_type=F32).astype(BF16)
    scale2 = HEAD_DIM ** -0.5 * LOG2_E
    for h in range(X_HEADS):
        cols = slice(h * HEAD_DIM, (h + 1) * HEAD_DIM)
        s = lax.dot_general(q_scr[:, cols], k_ref[0, :, cols], (((1,), (1,)), ((), ())),
                            preferred_element_type=F32) * scale2
        p = jnp.exp2(s - jnp.max(s, axis=-1, keepdims=True))
        l = jnp.sum(p, axis=-1, keepdims=True)
        o = jnp.dot(p.astype(BF16), v_ref[0, :, cols], preferred_element_type=F32) / l
        o_scr[:, cols] = o.astype(BF16)
    return xf + jnp.dot(o_scr[...], wo_ref[0], preferred_element_type=F32)


def _ffn_body(x_ref, g_ref, wgu_ref, wd_ref, fg_ref, *rest, ck, final, with_mem):
    if with_mem:
        xg_ref, wq_ref, k_ref, v_ref, wo_ref, o_ref, xn_ref, act_ref, xmid_ref, q_scr, o_scr = rest
        xmid_ref[...] = _mem_attention(x_ref[0], xg_ref, wq_ref, k_ref, v_ref, wo_ref, q_scr, o_scr)
        residual = lambda: xmid_ref[...]
    else:
        o_ref, xn_ref, act_ref = rest
        residual = lambda: x_ref[0]
    xn_ref[...] = _rms(residual(), g_ref[0]).astype(BF16)
    for c in range(D_FF // ck):
        xn = xn_ref[...]
        a = jnp.dot(xn, wgu_ref[0, :, c * ck:(c + 1) * ck], preferred_element_type=F32)
        u = jnp.dot(xn, wgu_ref[0, :, D_FF + c * ck:D_FF + (c + 1) * ck], preferred_element_type=F32)
        act_ref[:, c * ck:(c + 1) * ck] = (a * _sigmoid(a) * u).astype(BF16)
    y = residual() + 0.5 * jnp.dot(act_ref[...], wd_ref[0], preferred_element_type=F32)
    if final:
        y = _rms(y, fg_ref[...])
    o_ref[0] = y


def _ffn(x, norm, w_gu, w_down, final_g, layer, *, final, mem=None, tm=1024, ck=256):
    b, s, d = x.shape
    tm = min(tm, s)
    body = functools.partial(_ffn_body, ck=ck, final=final, with_mem=mem is not None)
    in_specs = [
        pl.BlockSpec((1, tm, d), lambda i, j: (i, j, 0)),
        _resident((1, 1, d), (layer, 0, 0)),
        _resident((1, d, 2 * D_FF), (layer, 0, 0)),
        _resident((1, D_FF, d), (layer, 0, 0)),
        _resident((1, d), (0, 0)),
    ]
    scratch = [pltpu.VMEM((tm, d), BF16), pltpu.VMEM((tm, D_FF), BF16)]
    operands = [x, norm, w_gu, w_down, final_g]
    if mem is not None:
        xnorm, wq, kmem, vmem, wo = mem
        kv = pl.BlockSpec((1, kmem.shape[1], X_W), lambda i, j: (i, 0, 0))
        in_specs += [_resident((1, 1, d), (layer, 0, 0)), _resident((1, d, X_W), (layer, 0, 0)),
                     kv, kv, _resident((1, X_W, d), (layer, 0, 0))]
        scratch += [pltpu.VMEM((tm, d), F32), pltpu.VMEM((tm, X_W), BF16), pltpu.VMEM((tm, X_W), BF16)]
        operands += [xnorm, wq, kmem, vmem, wo]
    return pl.pallas_call(
        body,
        out_shape=jax.ShapeDtypeStruct(x.shape, F32),
        grid=(b, s // tm),
        in_specs=in_specs,
        out_specs=pl.BlockSpec((1, tm, d), lambda i, j: (i, j, 0)),
        scratch_shapes=scratch,
        compiler_params=_params(2),
        name="ffn_mem" if mem is not None else "ffn",
    )(*operands)


def _pool_conv_steps(pw_ref, ps_ref, dw_ref, cb_ref, lg_ref, lb_ref, yb_ref, yc_ref,
                     pbuf, ubuf, sbuf, pooled_ref, *, ts, seq, tile, pace):
    chunk = min(CONV_CHUNK, ts)
    quarter = chunk // ROW_PHASES
    steps = []

    def rows_from(buf, g, start):
        return buf[g, pl.ds(HALO + start, quarter, stride=ROW_PHASES), :]

    def windowed(buf, g, row0, first, taps, weight):
        accs = [None] * ROW_PHASES
        for m in range(taps + ROW_PHASES - 1):
            v = rows_from(buf, g, row0 + first + m)
            for ph in range(ROW_PHASES):
                k = m - ph
                if 0 <= k < taps:
                    term = weight(k, v)
                    accs[ph] = term if accs[ph] is None else accs[ph] + term
        return accs

    def store_phases(g, row0, accs):
        for ph, acc in enumerate(accs):
            sbuf[g, pl.ds(row0 + ph, quarter, stride=ROW_PHASES), :] = acc

    def pool_group(g, kw):
        cols = slice(g * POOL_GROUP, (g + 1) * POOL_GROUP)
        for row0 in range(0, ts, chunk):
            store_phases(g, row0, windowed(pbuf, g, row0, -(kw // 2), kw, lambda k, v: v))
        pos = tile * ts + lax.broadcasted_iota(jnp.int32, (ts, 1), 0)
        cnt = (jnp.minimum(pos + kw // 2, seq) - jnp.maximum(pos - kw // 2, 0)).astype(F32)
        pooled_ref[:, cols] = (sbuf[g] / cnt - pbuf[g, HALO:HALO + ts]).astype(BF16)

    def pool_mix():
        for g in range(len(POOL_SIZES)):
            cols = slice(g * POOL_GROUP, (g + 1) * POOL_GROUP)
            mixed = jnp.dot(pooled_ref[:, cols], pw_ref[0, g], preferred_element_type=F32)
            yb_ref[0, :, cols] = (mixed * ps_ref[0, :, cols]).astype(yb_ref.dtype)

    def conv_pass(g, row0):
        cols = slice(g * HEAD_DIM, (g + 1) * HEAD_DIM)
        bias = cb_ref[0, :, cols]
        after = pace()
        if after is not None:
            bias = bias + _zero_after(after)
        accs = windowed(ubuf, g, row0, -(CONV_K // 2), CONV_K,
                        lambda k, v: dw_ref[0, k:k + 1, cols] * v)
        store_phases(g, row0, [acc + bias for acc in accs])

    def conv_norm():
        tot = sum(jnp.sum(sbuf[g], axis=-1, keepdims=True) for g in range(CONV_GROUPS))
        mu = tot / CONV_W
        sq = sum(jnp.sum((sbuf[g] - mu) ** 2, axis=-1, keepdims=True) for g in range(CONV_GROUPS))
        inv = lax.rsqrt(sq / CONV_W + EPS)
        for g in range(CONV_GROUPS):
            cols = slice(g * HEAD_DIM, (g + 1) * HEAD_DIM)
            y = (sbuf[g] - mu) * inv * lg_ref[0, :, cols] + lb_ref[0, :, cols]
            yc_ref[0, :, cols] = (y * _sigmoid(y)).astype(yc_ref.dtype)

    for g, kw in enumerate(POOL_SIZES):
        steps.append(functools.partial(pool_group, g, kw))
    for g in range(CONV_GROUPS):
        for row0 in range(0, ts, chunk):
            steps.append(functools.partial(conv_pass, g, row0))
    steps.append(conv_norm)
    steps.append(pool_mix)
    return steps


def _emit_interleaved(first, second):
    order = sorted([((k + 1) / len(first), 0, k) for k in range(len(first))]
                   + [((k + 0.5) / len(second), 1, k) for k in range(len(second))])
    for _, which, k in order:
        (first, second)[which][k]()


def _proj_body(x_ref, g_ref, w_ref, pw_ref, ps_ref, dw_ref, cb_ref, lg_ref, lb_ref,
               qkv0_ref, qkv1_ref, qkv2_ref, xno_ref, yb_ref, yc_ref,
               xn_ref, xn4_ref, xn16_ref, xs_ref, pbuf, ubuf, sbuf, pnew, unew, pooled_ref, *, seq):
    tm = xn_ref.shape[0]
    j = pl.program_id(1)
    n_tiles = pl.num_programs(1) - 1
    lane_tiles = xs_ref.shape[0]
    groups = range(CONV_GROUPS)
    body = slice(HALO, HALO + tm)
    tail = slice(HALO + tm, HALO + tm + HALO)

    def chunk(lhs_ref, base):
        return jnp.dot(lhs_ref[...], w_ref[0, :, base:base + ATTN_W], preferred_element_type=F32)

    def branch_steps(pace):
        return _pool_conv_steps(pw_ref, ps_ref, dw_ref, cb_ref, lg_ref, lb_ref, yb_ref, yc_ref,
                                pbuf, ubuf, sbuf, pooled_ref, ts=tm, seq=seq, tile=jnp.maximum(j - 1, 0),
                                pace=pace)

    @pl.when(j == 0)
    def _():
        pbuf[...] = jnp.zeros_like(pbuf)
        ubuf[...] = jnp.zeros_like(ubuf)

    @pl.when(j < n_tiles)
    def _():
        xn = _rms(x_ref[0], g_ref[0])
        xn_ref[...] = xn.astype(BF16)
        xno_ref[0] = xn_ref[...]
        xp = chunk(xn_ref, 3 * QKV_W)
        u = chunk(xn_ref, 3 * QKV_W + POOL_W) * _sigmoid(chunk(xn_ref, 3 * QKV_W + POOL_W + CONV_W))
        for g in groups:
            cols = slice(g * HEAD_DIM, (g + 1) * HEAD_DIM)
            pnew[g] = xp[:, cols]
            unew[g] = u[:, cols]
            pbuf[g, tail] = xp[0:HALO, cols]
            ubuf[g, tail] = u[0:HALO, cols]

        last = [None]

        def token_order(part):
            y = chunk(xn_ref, part * QKV_W).astype(BF16)
            qkv0_ref[0, 0, :, part * ATTN_W:(part + 1) * ATTN_W] = y
            last[0] = y[tm - 16:tm, ATTN_W - HEAD_DIM:ATTN_W]

        def regroup(g, lhs_ref):
            r = DIL_PAIRS[g][1]
            n = tm // r
            if g == 1:
                for t in range(lane_tiles):
                    xs_ref[t] = xn[:, t * HEAD_DIM:(t + 1) * HEAD_DIM]
            for c in range(r):
                for t in range(lane_tiles):
                    lhs_ref[c * n:(c + 1) * n, t * HEAD_DIM:(t + 1) * HEAD_DIM] = (
                        xs_ref[t, pl.ds(c, n, stride=r), :].astype(BF16))

        def class_major(g, lhs_ref, out_ref, part):
            r = DIL_PAIRS[g][1]
            n = tm // r
            if part == 0:
                regroup(g, lhs_ref)
            y = chunk(lhs_ref, part * QKV_W + g * ATTN_W).astype(BF16)
            last[0] = y[tm - 16:tm, ATTN_W - HEAD_DIM:ATTN_W]
            for c in range(r):
                out_ref[0, c, :, part * ATTN_W:(part + 1) * ATTN_W] = y[c * n:(c + 1) * n]

        mxu_steps = [functools.partial(token_order, part) for part in range(3)]
        for g, lhs_ref, out_ref in ((1, xn4_ref, qkv1_ref), (2, xn16_ref, qkv2_ref)):
            mxu_steps += [functools.partial(class_major, g, lhs_ref, out_ref, part) for part in range(3)]
        _emit_interleaved(branch_steps(lambda: last[0]), mxu_steps)

        for g in groups:
            pbuf[g, 0:HALO] = pbuf[g, tm:tm + HALO]
            ubuf[g, 0:HALO] = ubuf[g, tm:tm + HALO]
            pbuf[g, body] = pnew[g]
            ubuf[g, body] = unew[g]

    @pl.when(j == n_tiles)
    def _():
        for g in groups:
            pbuf[g, tail] = jnp.zeros((HALO, HEAD_DIM), F32)
            ubuf[g, tail] = jnp.zeros((HALO, HEAD_DIM), F32)
        for step in branch_steps(lambda: None):
            step()


def _proj(x, w, layer, *, tm=512):
    b, s, d = x.shape
    tm = min(tm, s)
    nt = s // tm
    cur = lambda j: jnp.minimum(j, nt - 1)
    row = lambda wd: pl.BlockSpec((1, tm, wd), lambda i, j: (i, cur(j), 0))
    late = lambda wd: pl.BlockSpec((1, tm, wd), lambda i, j: (i, jnp.maximum(j - 1, 0), 0))
    rs = [r for _, r in DIL_PAIRS]
    qkv_shape = lambda r: jax.ShapeDtypeStruct((b, r, s // r, 3 * ATTN_W), BF16)
    qkv_spec = lambda r: pl.BlockSpec((1, r, tm // r, 3 * ATTN_W), lambda i, j: (i, 0, cur(j), 0))
    vec = lambda wd: _resident((1, 1, wd), (layer, 0, 0))
    tile_f32 = lambda rows: pltpu.VMEM((CONV_GROUPS, rows, HEAD_DIM), F32)
    return pl.pallas_call(
        functools.partial(_proj_body, seq=s),
        out_shape=(*[qkv_shape(r) for r in rs],
                   jax.ShapeDtypeStruct((b, s, d), BF16),
                   jax.ShapeDtypeStruct((b, s, POOL_W), BF16),
                   jax.ShapeDtypeStruct((b, s, CONV_W), BF16)),
        grid=(b, nt + 1),
        in_specs=[row(d), vec(d),
                  _resident((1, d, PROJ_W), (layer, 0, 0)),
                  _resident((1, len(POOL_SIZES), POOL_GROUP, POOL_GROUP), (layer, 0, 0, 0)),
                  vec(POOL_W),
                  _resident((1, CONV_K, CONV_W), (layer, 0, 0)),
                  vec(CONV_W), vec(CONV_W), vec(CONV_W)],
        out_specs=(*[qkv_spec(r) for r in rs], row(d), late(POOL_W), late(CONV_W)),
        scratch_shapes=[pltpu.VMEM((tm, d), BF16), pltpu.VMEM((tm, d), BF16), pltpu.VMEM((tm, d), BF16),
                        pltpu.VMEM((d // HEAD_DIM, tm, HEAD_DIM), F32),
                        tile_f32(tm + 2 * HALO), tile_f32(tm + 2 * HALO), tile_f32(tm),
                        tile_f32(tm), tile_f32(tm), pltpu.VMEM((tm, POOL_W), BF16)],
        compiler_params=_params(2),
        name="in_proj",
    )(x, w["mix_norm"], w["w_in"], w["pool_w"], w["pool_scale"], w["conv_dw"], w["conv_b"],
      w["conv_ln_g"], w["conv_ln_b"])


def _attn_body(q_ref, kp_ref, kc_ref, kn_ref, vp_ref, vc_ref, vn_ref, bias_ref,
               o_ref, lse_ref, kedge, vedge, *, tl, seq_l):
    h64 = BAND_HALF
    n_sub = tl // Q_SUB
    win = 2 * Q_SUB
    row0 = pl.program_id(2) * tl
    scale2 = HEAD_DIM ** -0.5 * LOG2_E
    lane = lax.broadcasted_iota(jnp.int32, (Q_SUB, HEAD_DIM), 1)
    for cc in range(q_ref.shape[0]):
        for edge, prev, cur, nxt in ((kedge, kp_ref, kc_ref, kn_ref), (vedge, vp_ref, vc_ref, vn_ref)):
            if n_sub == 1:
                edge[cc, 0, 0:h64] = prev[cc]
                edge[cc, 0, h64:h64 + tl] = cur[cc]
                edge[cc, 0, h64 + tl:win] = nxt[cc]
            else:
                edge[cc, 0, 0:h64] = prev[cc]
                edge[cc, 0, h64:win] = cur[cc, 0:win - h64]
                edge[cc, 1, 0:win - h64] = cur[cc, tl - (win - h64):tl]
                edge[cc, 1, win - h64:win] = nxt[cc]

        def window(edge, cur, j, cols):
            if j == 0:
                return edge[cc, 0, :, cols]
            if j == n_sub - 1:
                return edge[cc, 1, :, cols]
            return cur[cc, j * Q_SUB - h64:j * Q_SUB - h64 + win, cols]

        for j in range(n_sub):
            rows = slice(j * Q_SUB, (j + 1) * Q_SUB)
            kmask = None
            if j == 0 or j == n_sub - 1:
                kpos = row0 + (j * Q_SUB - h64) + lax.broadcasted_iota(jnp.int32, (1, win), 1)
                kmask = jnp.where((kpos >= 0) & (kpos < seq_l), 0.0, NEG_INF).astype(F32)
            lse_blk = jnp.zeros((Q_SUB, HEAD_DIM), F32)
            for h in range(HEADS_PER_GROUP):
                cols = slice(h * HEAD_DIM, (h + 1) * HEAD_DIM)
                q = q_ref[cc, rows, cols]
                k = window(kedge, kc_ref, j, cols)
                v = window(vedge, vc_ref, j, cols)
                s = lax.dot_general(q, k, (((1,), (1,)), ((), ())), preferred_element_type=F32)
                s = s * scale2 + bias_ref[h]
                if kmask is not None:
                    s = s + kmask
                mx = jnp.max(s, axis=-1, keepdims=True)
                p = jnp.exp2(s - mx)
                l = jnp.sum(p, axis=-1, keepdims=True)
                o = jnp.dot(p.astype(BF16), v, preferred_element_type=F32) / l
                o_ref[cc, rows, cols] = o.astype(o_ref.dtype)
                lse_blk = jnp.where(lane == h, (mx + jnp.log2(l)) * LN_2, lse_blk)
            lse_ref[cc, rows, :] = lse_blk


def _attn_group(qkv, bias):
    b, r, sl, _ = qkv.shape
    tl = min(ATTN_TILE, sl)
    ncls = min(r, max(1, ATTN_TILE // sl))
    nh = sl // BAND_HALF
    per = tl // BAND_HALF
    cur = lambda part: pl.BlockSpec((None, ncls, tl, ATTN_W), lambda i, c, m: (i, c, m, part))
    prev = lambda part: pl.BlockSpec((None, ncls, BAND_HALF, ATTN_W),
                                     lambda i, c, m: (i, c, jnp.maximum(m * per - 1, 0), part))
    nxt = lambda part: pl.BlockSpec((None, ncls, BAND_HALF, ATTN_W),
                                    lambda i, c, m: (i, c, jnp.minimum((m + 1) * per, nh - 1), part))
    edge = pltpu.VMEM((ncls, 2, 2 * Q_SUB, ATTN_W), BF16)
    return pl.pallas_call(
        functools.partial(_attn_body, tl=tl, seq_l=sl),
        out_shape=(jax.ShapeDtypeStruct((b, r, sl, ATTN_W), BF16),
                   jax.ShapeDtypeStruct((b, r, sl, HEAD_DIM), F32)),
        grid=(b, r // ncls, sl // tl),
        in_specs=[cur(0), prev(1), cur(1), nxt(1), prev(2), cur(2), nxt(2),
                  _resident((HEADS_PER_GROUP, Q_SUB, 2 * Q_SUB), (0, 0, 0))],
        out_specs=(pl.BlockSpec((None, ncls, tl, ATTN_W), lambda i, c, m: (i, c, m, 0)),
                   pl.BlockSpec((None, ncls, tl, HEAD_DIM), lambda i, c, m: (i, c, m, 0))),
        scratch_shapes=[edge, edge],
        compiler_params=_params(3),
        name=f"dilated_attn_r{r}",
    )(qkv, qkv, qkv, qkv, qkv, qkv, qkv, bias)


def _t5_bucket(rel):
    half = REL_BUCKETS // 2
    n = jnp.abs(rel)
    nf = jnp.maximum(n, 1).astype(F32)
    large = REL_MAX_EXACT + (jnp.log(nf / REL_MAX_EXACT) / math.log(REL_MAX_DIST / REL_MAX_EXACT)
                             * (half - REL_MAX_EXACT)).astype(jnp.int32)
    large = jnp.minimum(large, half - 1)
    return jnp.where(rel > 0, half, 0) + jnp.where(n < REL_MAX_EXACT, n, large)


def _band_bias(rel_bias, g, r):
    delta = jnp.arange(2 * Q_SUB)[None, :] - BAND_HALF - jnp.arange(Q_SUB)[:, None]
    bucket = _t5_bucket(r * delta)
    table = rel_bias[:, g * HEADS_PER_GROUP:(g + 1) * HEADS_PER_GROUP].astype(F32)
    onehot = bucket[None, :, :, None] == jnp.arange(REL_BUCKETS)
    tbl = jnp.sum(jnp.where(onehot, table.T[:, None, None, :], 0.0), axis=-1)
    return jnp.where((jnp.abs(delta) <= BAND_HALF)[None], tbl * LOG2_E, NEG_INF)


def _merge_body(x_ref, xn_ref, o0, o1, o2, l0, l1, l2, yb_ref, yc_ref,
                wg_ref, wa_ref, wb_ref, wc_ref, wo_ref, out_ref, at_ref, obuf, lbuf):
    tm = at_ref.shape[0]
    d = D_MODEL
    n_g = len(DIL_PAIRS)

    def attn_head(h):
        for gi, (o, l, (_, r)) in enumerate(zip((o0, o1, o2), (l0, l1, l2), DIL_PAIRS)):
            for c in range(r):
                dst = pl.ds(c, tm // r, stride=r) if r > 1 else pl.ds(0, tm)
                if h == 0:
                    lbuf[gi, dst, :] = l[0, c]
                obuf[gi, h, dst, :] = o[0, c, :, h * HEAD_DIM:(h + 1) * HEAD_DIM].astype(F32)
        ls = [lbuf[gi, :, h:h + 1] for gi in range(n_g)]
        mx = jnp.maximum(jnp.maximum(ls[0], ls[1]), ls[2])
        es = [jnp.exp(t - mx) for t in ls]
        den = es[0] + es[1] + es[2]
        acc = sum((es[gi] / den) * obuf[gi, h] for gi in range(n_g))
        at_ref[:, h * HEAD_DIM:(h + 1) * HEAD_DIM] = acc.astype(BF16)

    for h in range(HEADS_PER_GROUP):
        attn_head(h)
    merged = None
    for bi, (br, w) in enumerate(((yb_ref[0], wb_ref), (yc_ref[0], wc_ref), (at_ref[...], wa_ref))):
        gi = (bi + 1) % 3
        gate = _sigmoid(jnp.dot(xn_ref[0], wg_ref[0, :, gi * d:(gi + 1) * d], preferred_element_type=F32))
        y = gate * jnp.dot(br, w[0], preferred_element_type=F32)
        merged = y if merged is None else merged + y
    out_ref[0] = x_ref[0] + jnp.dot(merged.astype(BF16), wo_ref[0], preferred_element_type=F32)


def _merge(x, xn, os_, lses, yb, yc, w, layer, *, tm=512):
    b, s, d = x.shape
    tm = min(tm, s)
    row = lambda wd: pl.BlockSpec((1, tm, wd), lambda i, j: (i, j, 0))
    cls = lambda r, wd: pl.BlockSpec((1, r, tm // r, wd), lambda i, j: (i, 0, j, 0))
    rs = [r for _, r in DIL_PAIRS]
    wbr = _resident((1, ATTN_W, d), (layer, 0, 0))
    return pl.pallas_call(
        _merge_body,
        out_shape=jax.ShapeDtypeStruct(x.shape, F32),
        grid=(b, s // tm),
        in_specs=[row(d), row(d),
                  *[cls(r, ATTN_W) for r in rs], *[cls(r, HEAD_DIM) for r in rs],
                  row(POOL_W), row(CONV_W),
                  pl.BlockSpec((1, d, 3 * d), lambda i, j: (layer, 0, PROJ_W // (3 * d)),
                               pipeline_mode=pl.Buffered(1)),
                  wbr, wbr, wbr, _resident((1, d, d), (layer, 0, 0))],
        out_specs=row(d),
        scratch_shapes=[pltpu.VMEM((tm, ATTN_W), BF16),
                        pltpu.VMEM((len(rs), HEADS_PER_GROUP, tm, HEAD_DIM), F32),
                        pltpu.VMEM((len(rs), tm, HEAD_DIM), F32)],
        compiler_params=_params(2),
        name="merge_out",
    )(x, xn, *os_, *lses, yb, yc, w["w_in"], w["w_br_attn"], w["w_br_pool"], w["w_br_conv"], w["w_out"])


def _memkv_body(m_ref, g_ref, w_ref, k_ref, v_ref):
    mn = _rms(m_ref[0], g_ref[0]).astype(BF16)
    kv = jnp.dot(mn, w_ref[0], preferred_element_type=F32)
    k_ref[0] = kv[:, :X_W].astype(BF16)
    v_ref[0] = kv[:, X_W:].astype(BF16)


def _memkv(mem, norm, wkv, layer):
    b, m, d = mem.shape
    out = pl.BlockSpec((1, m, X_W), lambda i: (i, 0, 0))
    return pl.pallas_call(
        _memkv_body,
        out_shape=(jax.ShapeDtypeStruct((b, m, X_W), BF16),) * 2,
        grid=(b,),
        in_specs=[pl.BlockSpec((1, m, d), lambda i: (i, 0, 0)),
                  _resident((1, 1, d), (layer, 0, 0)),
                  _resident((1, d, 2 * X_W), (layer, 0, 0))],
        out_specs=(out, out),
        compiler_params=_params(1),
        name="mem_kv",
    )(mem, norm, wkv)


def _trunk(x, mem, w, biases, depth):
    for l in range(depth):
        x = _ffn(x, w["ffn1_norm"], w["ffn1_w_gu"], w["ffn1_w_down"], w["final_norm"], l, final=False)
        *qkvs, xn, yb, yc = _proj(x, w, l)
        outs = [_attn_group(qkv, bias) for qkv, bias in zip(qkvs, biases)]
        x = _merge(x, xn, [o for o, _ in outs], [s for _, s in outs], yb, yc, w, l)
        kmem, vmem = _memkv(mem, w["mem_norm"], w["xattn_wkv"], l)
        x = _ffn(x, w["ffn2_norm"], w["ffn2_w_gu"], w["ffn2_w_down"], w["final_norm"], l,
                 final=(l == depth - 1),
                 mem=(w["xattn_norm"], w["xattn_wq"], kmem, vmem, w["xattn_wo"]))
    return x


def kernel(x_prompt, x_sample, mem_prompt, mem_sample, rel_bias, ffn1_norm, ffn1_w_gu, ffn1_w_down, mix_norm, w_in, pool_w, pool_scale, conv_dw, conv_b, conv_ln_g, conv_ln_b, w_br_attn, w_br_pool, w_br_conv, w_out, xattn_norm, mem_norm, xattn_wq, xattn_wkv, xattn_wo, ffn2_norm, ffn2_w_gu, ffn2_w_down, final_norm):
    depth = w_in.shape[0]
    mat = lambda t: t.astype(BF16)
    vec = lambda t: t.reshape(depth, 1, t.shape[-1])
    w = dict(
        ffn1_norm=vec(ffn1_norm), ffn1_w_gu=mat(ffn1_w_gu), ffn1_w_down=mat(ffn1_w_down),
        mix_norm=vec(mix_norm), w_in=mat(w_in), pool_w=mat(pool_w), pool_scale=vec(pool_scale),
        conv_dw=conv_dw, conv_b=vec(conv_b), conv_ln_g=vec(conv_ln_g), conv_ln_b=vec(conv_ln_b),
        w_br_attn=mat(w_br_attn), w_br_pool=mat(w_br_pool), w_br_conv=mat(w_br_conv), w_out=mat(w_out),
        xattn_norm=vec(xattn_norm), mem_norm=vec(mem_norm), xattn_wq=mat(xattn_wq),
        xattn_wkv=mat(xattn_wkv), xattn_wo=mat(xattn_wo),
        ffn2_norm=vec(ffn2_norm), ffn2_w_gu=mat(ffn2_w_gu), ffn2_w_down=mat(ffn2_w_down),
        final_norm=final_norm.reshape(1, -1),
    )
    biases = [_band_bias(rel_bias, g, r) for g, (_, r) in enumerate(DIL_PAIRS)]
    y_prompt = _trunk(x_prompt, mem_prompt, w, biases, depth)
    y_sample = _trunk(x_sample, mem_sample, w, biases, depth)
    return (y_prompt, y_sample)
```

```python
import functools
import math

import jax
import jax.numpy as jnp
from jax import lax
from jax.experimental import pallas as pl
from jax.experimental.pallas import tpu as pltpu

F32 = jnp.float32
BF16 = jnp.bfloat16

D_MODEL = 1024
D_FF = 2816
DIL_PAIRS = ((128, 1), (512, 4), (2048, 16))
HEADS_PER_GROUP = 4
N_HEADS_A = len(DIL_PAIRS) * HEADS_PER_GROUP
HEAD_DIM = 128
QKV_W = N_HEADS_A * HEAD_DIM
ATTN_W = HEADS_PER_GROUP * HEAD_DIM
BAND_HALF = 64
REL_BUCKETS = 32
REL_MAX_EXACT = 8
REL_MAX_DIST = 1024
POOL_SIZES = (2, 4, 8, 16)
POOL_GROUP = 128
POOL_W = len(POOL_SIZES) * POOL_GROUP
CONV_W = 512
CONV_K = 31
CONV_GROUPS = CONV_W // HEAD_DIM
X_HEADS = 4
X_W = X_HEADS * HEAD_DIM
PROJ_W = 3 * QKV_W + POOL_W + 2 * CONV_W
EPS = 1e-6
NEG_INF = -1e30
LOG2_E = math.log2(math.e)
LN_2 = math.log(2.0)

HALO = 16
CONV_CHUNK = 256
ROW_PHASES = 4
GATE_CHUNK = 256
Q_SUB = 128
ATTN_TILE = 4096
VMEM_LIMIT = 56 * 1024 * 1024


def _params(n_axes):
    return pltpu.CompilerParams(dimension_semantics=("arbitrary",) * n_axes,
                                vmem_limit_bytes=VMEM_LIMIT)


def _resident(shape, index):
    return pl.BlockSpec(shape, lambda *_: index, pipeline_mode=pl.Buffered(1))


def _rms(xf, g):
    ms = jnp.mean(xf * xf, axis=-1, keepdims=True)
    return xf * lax.rsqrt(ms + EPS) * g


def _sigmoid(x):
    return 1.0 / (1.0 + jnp.exp2(x * -LOG2_E))


def _zero_after(v):
    bits = pltpu.bitcast(v, jnp.uint32)
    return pltpu.bitcast((bits >> 16) >> 16, F32)


def _mem_attention(xf, g_ref, wq_ref, k_ref, v_ref, wo_ref, q_scr, o_scr):
    xn = _rms(xf, g_ref[0]).astype(BF16)
    q_scr[...] = jnp.dot(xn, wq_ref[0], preferred_element_type=F32).astype(BF16)
    scale2 = HEAD_DIM ** -0.5 * LOG2_E
    for h in range(X_HEADS):
        cols = slice(h * HEAD_DIM, (h + 1) * HEAD_DIM)
        s = lax.dot_general(q_scr[:, cols], k_ref[0, :, cols], (((1,), (1,)), ((), ())),
                            preferred_element_type=F32) * scale2
        p = jnp.exp2(s - jnp.max(s, axis=-1, keepdims=True))
        l = jnp.sum(p, axis=-1, keepdims=True)
        o = jnp.dot(p.astype(BF16), v_ref[0, :, cols], preferred_element_type=F32) / l
        o_scr[:, cols] = o.astype(BF16)
    return xf + jnp.dot(o_scr[...], wo_ref[0], preferred_element_type=F32)


def _ffn_body(x_ref, g_ref, wgu_ref, wd_ref, fg_ref, *rest, ck, final, with_mem):
    if with_mem:
        xg_ref, wq_ref, k_ref, v_ref, wo_ref, o_ref, xn_ref, act_ref, xmid_ref, q_scr, o_scr = rest
        xmid_ref[...] = _mem_attention(x_ref[0], xg_ref, wq_ref, k_ref, v_ref, wo_ref, q_scr, o_scr)
        residual = lambda: xmid_ref[...]
    else:
        o_ref, xn_ref, act_ref = rest
        residual = lambda: x_ref[0]
    xn_ref[...] = _rms(residual(), g_ref[0]).astype(BF16)
    for c in range(D_FF // ck):
        xn = xn_ref[...]
        a = jnp.dot(xn, wgu_ref[0, :, c * ck:(c + 1) * ck], preferred_element_type=F32)
        u = jnp.dot(xn, wgu_ref[0, :, D_FF + c * ck:D_FF + (c + 1) * ck], preferred_element_type=F32)
        act_ref[:, c * ck:(c + 1) * ck] = (a * _sigmoid(a) * u).astype(BF16)
    y = residual() + 0.5 * jnp.dot(act_ref[...], wd_ref[0], preferred_element_type=F32)
    if final:
        y = _rms(y, fg_ref[...])
    o_ref[0] = y


def _ffn(x, norm, w_gu, w_down, final_g, layer, *, final, mem=None, tm=1024, ck=256):
    b, s, d = x.shape
    tm = min(tm, s)
    body = functools.partial(_ffn_body, ck=ck, final=final, with_mem=mem is not None)
    in_specs = [
        pl.BlockSpec((1, tm, d), lambda i, j: (i, j, 0)),
        _resident((1, 1, d), (layer, 0, 0)),
        _resident((1, d, 2 * D_FF), (layer, 0, 0)),
        _resident((1, D_FF, d), (layer, 0, 0)),
        _resident((1, d), (0, 0)),
    ]
    scratch = [pltpu.VMEM((tm, d), BF16), pltpu.VMEM((tm, D_FF), BF16)]
    operands = [x, norm, w_gu, w_down, final_g]
    if mem is not None:
        xnorm, wq, kmem, vmem, wo = mem
        kv = pl.BlockSpec((1, kmem.shape[1], X_W), lambda i, j: (i, 0, 0))
        in_specs += [_resident((1, 1, d), (layer, 0, 0)), _resident((1, d, X_W), (layer, 0, 0)),
                     kv, kv, _resident((1, X_W, d), (layer, 0, 0))]
        scratch += [pltpu.VMEM((tm, d), F32), pltpu.VMEM((tm, X_W), BF16), pltpu.VMEM((tm, X_W), BF16)]
        operands += [xnorm, wq, kmem, vmem, wo]
    return pl.pallas_call(
        body,
        out_shape=jax.ShapeDtypeStruct(x.shape, F32),
        grid=(b, s // tm),
        in_specs=in_specs,
        out_specs=pl.BlockSpec((1, tm, d), lambda i, j: (i, j, 0)),
        scratch_shapes=scratch,
        compiler_params=_params(2),
        name="ffn_mem" if mem is not None else "ffn",
    )(*operands)


def _proj_body(x_ref, g_ref, w_ref, qkv0_ref, qkv1_ref, qkv2_ref, xp_ref, xc_ref, xno_ref,
               xn_ref, xn4_ref, xn16_ref, xs_ref):
    tm = xn_ref.shape[0]
    xn = _rms(x_ref[0], g_ref[0])
    xn_ref[...] = xn.astype(BF16)
    xno_ref[0] = xn_ref[...]

    def chunk(lhs_ref, base):
        return jnp.dot(lhs_ref[...], w_ref[0, :, base:base + ATTN_W], preferred_element_type=F32)

    for part in range(3):
        qkv0_ref[0, 0, :, part * ATTN_W:(part + 1) * ATTN_W] = chunk(xn_ref, part * QKV_W).astype(BF16)
    xp_ref[0] = chunk(xn_ref, 3 * QKV_W)
    xc_ref[0] = chunk(xn_ref, 3 * QKV_W + POOL_W) * _sigmoid(chunk(xn_ref, 3 * QKV_W + POOL_W + CONV_W))

    for t in range(xs_ref.shape[0]):
        xs_ref[t] = xn[:, t * HEAD_DIM:(t + 1) * HEAD_DIM]
    for g, (lhs_ref, out_ref) in ((1, (xn4_ref, qkv1_ref)), (2, (xn16_ref, qkv2_ref))):
        r = DIL_PAIRS[g][1]
        n = tm // r
        for c in range(r):
            for t in range(xs_ref.shape[0]):
                lhs_ref[c * n:(c + 1) * n, t * HEAD_DIM:(t + 1) * HEAD_DIM] = (
                    xs_ref[t, pl.ds(c, n, stride=r), :].astype(BF16))
        for part in range(3):
            y = chunk(lhs_ref, part * QKV_W + g * ATTN_W).astype(BF16)
            for c in range(r):
                out_ref[0, c, :, part * ATTN_W:(part + 1) * ATTN_W] = y[c * n:(c + 1) * n]


def _proj(x, norm, w_in, layer, *, tm=512):
    b, s, d = x.shape
    tm = min(tm, s)
    row = lambda w: pl.BlockSpec((1, tm, w), lambda i, j: (i, j, 0))
    rs = [r for _, r in DIL_PAIRS]
    qkv_shape = lambda r: jax.ShapeDtypeStruct((b, r, s // r, 3 * ATTN_W), BF16)
    qkv_spec = lambda r: pl.BlockSpec((1, r, tm // r, 3 * ATTN_W), lambda i, j: (i, 0, j, 0))
    return pl.pallas_call(
        _proj_body,
        out_shape=(*[qkv_shape(r) for r in rs],
                   jax.ShapeDtypeStruct((b, s, POOL_W), F32),
                   jax.ShapeDtypeStruct((b, s, CONV_W), F32),
                   jax.ShapeDtypeStruct((b, s, d), BF16)),
        grid=(b, s // tm),
        in_specs=[row(d), _resident((1, 1, d), (layer, 0, 0)),
                  _resident((1, d, PROJ_W), (layer, 0, 0))],
        out_specs=(*[qkv_spec(r) for r in rs], row(POOL_W), row(CONV_W), row(d)),
        scratch_shapes=[pltpu.VMEM((tm, d), BF16), pltpu.VMEM((tm, d), BF16), pltpu.VMEM((tm, d), BF16),
                        pltpu.VMEM((d // HEAD_DIM, tm, HEAD_DIM), F32)],
        compiler_params=_params(2),
        name="in_proj",
    )(x, norm, w_in)


def _attn_body(q_ref, kp_ref, kc_ref, kn_ref, vp_ref, vc_ref, vn_ref, bias_ref,
               o_ref, lse_ref, kedge, vedge, *, tl, seq_l):
    h64 = BAND_HALF
    n_sub = tl // Q_SUB
    win = 2 * Q_SUB
    row0 = pl.program_id(2) * tl
    scale2 = HEAD_DIM ** -0.5 * LOG2_E
    lane = lax.broadcasted_iota(jnp.int32, (Q_SUB, HEAD_DIM), 1)
    for cc in range(q_ref.shape[0]):
        for edge, prev, cur, nxt in ((kedge, kp_ref, kc_ref, kn_ref), (vedge, vp_ref, vc_ref, vn_ref)):
            if n_sub == 1:
                edge[cc, 0, 0:h64] = prev[cc]
                edge[cc, 0, h64:h64 + tl] = cur[cc]
                edge[cc, 0, h64 + tl:win] = nxt[cc]
            else:
                edge[cc, 0, 0:h64] = prev[cc]
                edge[cc, 0, h64:win] = cur[cc, 0:win - h64]
                edge[cc, 1, 0:win - h64] = cur[cc, tl - (win - h64):tl]
                edge[cc, 1, win - h64:win] = nxt[cc]

        def window(edge, cur, j, cols):
            if j == 0:
                return edge[cc, 0, :, cols]
            if j == n_sub - 1:
                return edge[cc, 1, :, cols]
            return cur[cc, j * Q_SUB - h64:j * Q_SUB - h64 + win, cols]

        for j in range(n_sub):
            rows = slice(j * Q_SUB, (j + 1) * Q_SUB)
            kmask = None
            if j == 0 or j == n_sub - 1:
                kpos = row0 + (j * Q_SUB - h64) + lax.broadcasted_iota(jnp.int32, (1, win), 1)
                kmask = jnp.where((kpos >= 0) & (kpos < seq_l), 0.0, NEG_INF).astype(F32)
            lse_blk = jnp.zeros((Q_SUB, HEAD_DIM), F32)
            for h in range(HEADS_PER_GROUP):
                cols = slice(h * HEAD_DIM, (h + 1) * HEAD_DIM)
                q = q_ref[cc, rows, cols]
                k = window(kedge, kc_ref, j, cols)
                v = window(vedge, vc_ref, j, cols)
                s = lax.dot_general(q, k, (((1,), (1,)), ((), ())), preferred_element_type=F32)
                s = s * scale2 + bias_ref[h]
                if kmask is not None:
                    s = s + kmask
                mx = jnp.max(s, axis=-1, keepdims=True)
                p = jnp.exp2(s - mx)
                l = jnp.sum(p, axis=-1, keepdims=True)
                o = jnp.dot(p.astype(BF16), v, preferred_element_type=F32) / l
                o_ref[cc, rows, cols] = o.astype(o_ref.dtype)
                lse_blk = jnp.where(lane == h, (mx + jnp.log2(l)) * LN_2, lse_blk)
            lse_ref[cc, rows, :] = lse_blk


def _attn_group(qkv, bias):
    b, r, sl, _ = qkv.shape
    tl = min(ATTN_TILE, sl)
    ncls = min(r, max(1, ATTN_TILE // sl))
    nh = sl // BAND_HALF
    per = tl // BAND_HALF
    cur = lambda part: pl.BlockSpec((None, ncls, tl, ATTN_W), lambda i, c, m: (i, c, m, part))
    prev = lambda part: pl.BlockSpec((None, ncls, BAND_HALF, ATTN_W),
                                     lambda i, c, m: (i, c, jnp.maximum(m * per - 1, 0), part))
    nxt = lambda part: pl.BlockSpec((None, ncls, BAND_HALF, ATTN_W),
                                    lambda i, c, m: (i, c, jnp.minimum((m + 1) * per, nh - 1), part))
    edge = pltpu.VMEM((ncls, 2, 2 * Q_SUB, ATTN_W), BF16)
    return pl.pallas_call(
        functools.partial(_attn_body, tl=tl, seq_l=sl),
        out_shape=(jax.ShapeDtypeStruct((b, r, sl, ATTN_W), BF16),
                   jax.ShapeDtypeStruct((b, r, sl, HEAD_DIM), F32)),
        grid=(b, r // ncls, sl // tl),
        in_specs=[cur(0), prev(1), cur(1), nxt(1), prev(2), cur(2), nxt(2),
                  _resident((HEADS_PER_GROUP, Q_SUB, 2 * Q_SUB), (0, 0, 0))],
        out_specs=(pl.BlockSpec((None, ncls, tl, ATTN_W), lambda i, c, m: (i, c, m, 0)),
                   pl.BlockSpec((None, ncls, tl, HEAD_DIM), lambda i, c, m: (i, c, m, 0))),
        scratch_shapes=[edge, edge],
        compiler_params=_params(3),
        name=f"dilated_attn_r{r}",
    )(qkv, qkv, qkv, qkv, qkv, qkv, qkv, bias)


def _t5_bucket(rel):
    half = REL_BUCKETS // 2
    n = jnp.abs(rel)
    nf = jnp.maximum(n, 1).astype(F32)
    large = REL_MAX_EXACT + (jnp.log(nf / REL_MAX_EXACT) / math.log(REL_MAX_DIST / REL_MAX_EXACT)
                             * (half - REL_MAX_EXACT)).astype(jnp.int32)
    large = jnp.minimum(large, half - 1)
    return jnp.where(rel > 0, half, 0) + jnp.where(n < REL_MAX_EXACT, n, large)


def _band_bias(rel_bias, g, r):
    delta = jnp.arange(2 * Q_SUB)[None, :] - BAND_HALF - jnp.arange(Q_SUB)[:, None]
    bucket = _t5_bucket(r * delta)
    table = rel_bias[:, g * HEADS_PER_GROUP:(g + 1) * HEADS_PER_GROUP].astype(F32)
    onehot = bucket[None, :, :, None] == jnp.arange(REL_BUCKETS)
    tbl = jnp.sum(jnp.where(onehot, table.T[:, None, None, :], 0.0), axis=-1)
    return jnp.where((jnp.abs(delta) <= BAND_HALF)[None], tbl * LOG2_E, NEG_INF)


def _pool_conv_steps(xp_p, xp_c, xp_n, xc_p, xc_c, xc_n, pw_ref, ps_ref, dw_ref, cb_ref, lg_ref, lb_ref,
                     yb_ref, yc_ref, pbuf, ubuf, sbuf, *, ts, seq, pace):
    i = pl.program_id(1)
    has_prev = (i > 0).astype(F32)
    has_next = (i < pl.num_programs(1) - 1).astype(F32)
    chunk = min(CONV_CHUNK, ts)
    quarter = chunk // ROW_PHASES
    steps = []

    def fill(buf, g, prev, cur, nxt):
        buf[g, 0:HALO] = prev * has_prev
        buf[g, HALO:HALO + ts] = cur
        buf[g, HALO + ts:HALO + ts + HALO] = nxt * has_next

    def rows_from(buf, g, start):
        return buf[g, pl.ds(HALO + start, quarter, stride=ROW_PHASES), :]

    def windowed(buf, g, row0, first, taps, weight):
        accs = [None] * ROW_PHASES
        for m in range(taps + ROW_PHASES - 1):
            v = rows_from(buf, g, row0 + first + m)
            for ph in range(ROW_PHASES):
                k = m - ph
                if 0 <= k < taps:
                    term = weight(k, v)
                    accs[ph] = term if accs[ph] is None else accs[ph] + term
        return accs

    def store_phases(g, row0, accs):
        for ph, acc in enumerate(accs):
            sbuf[g, pl.ds(row0 + ph, quarter, stride=ROW_PHASES), :] = acc

    def pool_group(g, kw):
        cols = slice(g * POOL_GROUP, (g + 1) * POOL_GROUP)
        fill(pbuf, g, xp_p[0, :, cols], xp_c[0, :, cols], xp_n[0, :, cols])
        for row0 in range(0, ts, chunk):
            store_phases(g, row0, windowed(pbuf, g, row0, -(kw // 2), kw, lambda k, v: v))
        pos = i * ts + lax.broadcasted_iota(jnp.int32, (ts, 1), 0)
        cnt = (jnp.minimum(pos + kw // 2, seq) - jnp.maximum(pos - kw // 2, 0)).astype(F32)
        pooled = (sbuf[g] / cnt - xp_c[0, :, cols]).astype(BF16)
        mixed = jnp.dot(pooled, pw_ref[0, g], preferred_element_type=F32)
        yb_ref[:, cols] = (mixed * ps_ref[0, :, cols]).astype(yb_ref.dtype)

    for g, kw in enumerate(POOL_SIZES):
        steps.append(functools.partial(pool_group, g, kw))

    def conv_fill(g):
        cols = slice(g * HEAD_DIM, (g + 1) * HEAD_DIM)
        fill(ubuf, g, xc_p[0, :, cols], xc_c[0, :, cols], xc_n[0, :, cols])

    def conv_pass(g, row0):
        cols = slice(g * HEAD_DIM, (g + 1) * HEAD_DIM)
        bias = cb_ref[0, :, cols]
        after = pace()
        if after is not None:
            bias = bias + _zero_after(after)
        accs = windowed(ubuf, g, row0, -(CONV_K // 2), CONV_K,
                        lambda k, v: dw_ref[0, k:k + 1, cols] * v)
        store_phases(g, row0, [acc + bias for acc in accs])

    def conv_norm():
        tot = sum(jnp.sum(sbuf[g], axis=-1, keepdims=True) for g in range(CONV_GROUPS))
        mu = tot / CONV_W
        sq = sum(jnp.sum((sbuf[g] - mu) ** 2, axis=-1, keepdims=True) for g in range(CONV_GROUPS))
        inv = lax.rsqrt(sq / CONV_W + EPS)
        for g in range(CONV_GROUPS):
            cols = slice(g * HEAD_DIM, (g + 1) * HEAD_DIM)
            y = (sbuf[g] - mu) * inv * lg_ref[0, :, cols] + lb_ref[0, :, cols]
            yc_ref[:, cols] = (y * _sigmoid(y)).astype(yc_ref.dtype)

    for g in range(CONV_GROUPS):
        steps.append(functools.partial(conv_fill, g))
        for row0 in range(0, ts, chunk):
            steps.append(functools.partial(conv_pass, g, row0))
    steps.append(conv_norm)
    return steps


def _emit_interleaved(first, second):
    order = sorted([((k + 1) / len(first), 0, k) for k in range(len(first))]
                   + [((k + 0.5) / len(second), 1, k) for k in range(len(second))])
    for _, which, k in order:
        (first, second)[which][k]()


def _merge_body(x_ref, xn_ref, o0, o1, o2, l0, l1, l2, xp_p, xp_c, xp_n, xc_p, xc_c, xc_n,
                pw_ref, ps_ref, dw_ref, cb_ref, lg_ref, lb_ref,
                wg_ref, wa_ref, wb_ref, wc_ref, wo_ref, out_ref,
                at_ref, yb_ref, yc_ref, gate_ref, obuf, lbuf, pbuf, ubuf, sbuf, *, seq):
    tm = at_ref.shape[0]
    d = D_MODEL
    n_g = len(DIL_PAIRS)

    def attn_head(h):
        for gi, (o, l, (_, r)) in enumerate(zip((o0, o1, o2), (l0, l1, l2), DIL_PAIRS)):
            for c in range(r):
                dst = pl.ds(c, tm // r, stride=r) if r > 1 else pl.ds(0, tm)
                if h == 0:
                    lbuf[gi, dst, :] = l[0, c]
                obuf[gi, h, dst, :] = o[0, c, :, h * HEAD_DIM:(h + 1) * HEAD_DIM].astype(F32)
        ls = [lbuf[gi, :, h:h + 1] for gi in range(n_g)]
        mx = jnp.maximum(jnp.maximum(ls[0], ls[1]), ls[2])
        es = [jnp.exp(t - mx) for t in ls]
        den = es[0] + es[1] + es[2]
        acc = sum((es[gi] / den) * obuf[gi, h] for gi in range(n_g))
        at_ref[:, h * HEAD_DIM:(h + 1) * HEAD_DIM] = acc.astype(BF16)

    last_gate = [None]

    def gate_chunk(bi, c):
        z = jnp.dot(xn_ref[0], wg_ref[0, :, bi * d + c * GATE_CHUNK:bi * d + (c + 1) * GATE_CHUNK],
                    preferred_element_type=F32)
        gate_ref[bi, :, c * GATE_CHUNK:(c + 1) * GATE_CHUNK] = z
        last_gate[0] = z[tm - 1:tm, GATE_CHUNK - HEAD_DIM:GATE_CHUNK]

    vpu_steps = [functools.partial(attn_head, h) for h in range(HEADS_PER_GROUP)]
    vpu_steps += _pool_conv_steps(xp_p, xp_c, xp_n, xc_p, xc_c, xc_n, pw_ref, ps_ref, dw_ref, cb_ref,
                                  lg_ref, lb_ref, yb_ref, yc_ref, pbuf, ubuf, sbuf, ts=tm, seq=seq,
                                  pace=lambda: last_gate[0])
    mxu_steps = [functools.partial(gate_chunk, bi, c) for bi in range(3) for c in range(d // GATE_CHUNK)]
    _emit_interleaved(vpu_steps, mxu_steps)

    merged = None
    for bi, (br, w) in enumerate(((at_ref, wa_ref), (yb_ref, wb_ref), (yc_ref, wc_ref))):
        y = _sigmoid(gate_ref[bi]) * jnp.dot(br[...], w[0], preferred_element_type=F32)
        merged = y if merged is None else merged + y
    out_ref[0] = x_ref[0] + jnp.dot(merged.astype(BF16), wo_ref[0], preferred_element_type=F32)


def _merge(x, xn, os_, lses, xp, xc, w, layer, *, tm=512):
    b, s, d = x.shape
    tm = min(tm, s)
    per = tm // HALO
    nh = s // HALO
    row = lambda wd: pl.BlockSpec((1, tm, wd), lambda i, j: (i, j, 0))
    cls = lambda r, wd: pl.BlockSpec((1, r, tm // r, wd), lambda i, j: (i, 0, j, 0))

    def halo_specs(wd):
        return (pl.BlockSpec((1, HALO, wd), lambda i, j: (i, jnp.maximum(j * per - 1, 0), 0)),
                row(wd),
                pl.BlockSpec((1, HALO, wd), lambda i, j: (i, jnp.minimum((j + 1) * per, nh - 1), 0)))

    rs = [r for _, r in DIL_PAIRS]
    vec = lambda wd: _resident((1, 1, wd), (layer, 0, 0))
    wbr = _resident((1, ATTN_W, d), (layer, 0, 0))
    tile_f32 = lambda groups, rows: pltpu.VMEM((groups, rows, HEAD_DIM), F32)
    return pl.pallas_call(
        functools.partial(_merge_body, seq=s),
        out_shape=jax.ShapeDtypeStruct(x.shape, F32),
        grid=(b, s // tm),
        in_specs=[row(d), row(d),
                  *[cls(r, ATTN_W) for r in rs], *[cls(r, HEAD_DIM) for r in rs],
                  *halo_specs(POOL_W), *halo_specs(CONV_W),
                  _resident((1, len(POOL_SIZES), POOL_GROUP, POOL_GROUP), (layer, 0, 0, 0)),
                  vec(POOL_W),
                  _resident((1, CONV_K, CONV_W), (layer, 0, 0)),
                  vec(CONV_W), vec(CONV_W), vec(CONV_W),
                  pl.BlockSpec((1, d, 3 * d), lambda i, j: (layer, 0, PROJ_W // (3 * d)),
                               pipeline_mode=pl.Buffered(1)),
                  wbr, wbr, wbr, _resident((1, d, d), (layer, 0, 0))],
        out_specs=row(d),
        scratch_shapes=[pltpu.VMEM((tm, ATTN_W), BF16), pltpu.VMEM((tm, POOL_W), BF16),
                        pltpu.VMEM((tm, CONV_W), BF16),
                        pltpu.VMEM((3, tm, d), F32),
                        pltpu.VMEM((len(rs), HEADS_PER_GROUP, tm, HEAD_DIM), F32),
                        tile_f32(len(rs), tm),
                        tile_f32(len(POOL_SIZES), tm + 2 * HALO),
                        tile_f32(CONV_GROUPS, tm + 2 * HALO),
                        tile_f32(CONV_GROUPS, tm)],
        compiler_params=_params(2),
        name="merge_out",
    )(x, xn, *os_, *lses, xp, xp, xp, xc, xc, xc,
      w["pool_w"], w["pool_scale"], w["conv_dw"], w["conv_b"], w["conv_ln_g"], w["conv_ln_b"],
      w["w_in"], w["w_br_attn"], w["w_br_pool"], w["w_br_conv"], w["w_out"])


def _memkv_body(m_ref, g_ref, w_ref, k_ref, v_ref):
    mn = _rms(m_ref[0], g_ref[0]).astype(BF16)
    kv = jnp.dot(mn, w_ref[0], preferred_element_type=F32)
    k_ref[0] = kv[:, :X_W].astype(BF16)
    v_ref[0] = kv[:, X_W:].astype(BF16)


def _memkv(mem, norm, wkv, layer):
    b, m, d = mem.shape
    out = pl.BlockSpec((1, m, X_W), lambda i: (i, 0, 0))
    return pl.pallas_call(
        _memkv_body,
        out_shape=(jax.ShapeDtypeStruct((b, m, X_W), BF16),) * 2,
        grid=(b,),
        in_specs=[pl.BlockSpec((1, m, d), lambda i: (i, 0, 0)),
                  _resident((1, 1, d), (layer, 0, 0)),
                  _resident((1, d, 2 * X_W), (layer, 0, 0))],
        out_specs=(out, out),
        compiler_params=_params(1),
        name="mem_kv",
    )(mem, norm, wkv)


def _trunk(x, mem, w, biases, depth):
    for l in range(depth):
        x = _ffn(x, w["ffn1_norm"], w["ffn1_w_gu"], w["ffn1_w_down"], w["final_norm"], l, final=False)
        *qkvs, xp, xc, xn = _proj(x, w["mix_norm"], w["w_in"], l)
        outs = [_attn_group(qkv, bias) for qkv, bias in zip(qkvs, biases)]
        x = _merge(x, xn, [o for o, _ in outs], [s for _, s in outs], xp, xc, w, l)
        kmem, vmem = _memkv(mem, w["mem_norm"], w["xattn_wkv"], l)
        x = _ffn(x, w["ffn2_norm"], w["ffn2_w_gu"], w["ffn2_w_down"], w["final_norm"], l,
                 final=(l == depth - 1),
                 mem=(w["xattn_norm"], w["xattn_wq"], kmem, vmem, w["xattn_wo"]))
    return x


def kernel(x_prompt, x_sample, mem_prompt, mem_sample, rel_bias, ffn1_norm, ffn1_w_gu, ffn1_w_down, mix_norm, w_in, pool_w, pool_scale, conv_dw, conv_b, conv_ln_g, conv_ln_b, w_br_attn, w_br_pool, w_br_conv, w_out, xattn_norm, mem_norm, xattn_wq, xattn_wkv, xattn_wo, ffn2_norm, ffn2_w_gu, ffn2_w_down, final_norm):
    depth = w_in.shape[0]
    mat = lambda t: t.astype(BF16)
    vec = lambda t: t.reshape(depth, 1, t.shape[-1])
    w = dict(
        ffn1_norm=vec(ffn1_norm), ffn1_w_gu=mat(ffn1_w_gu), ffn1_w_down=mat(ffn1_w_down),
        mix_norm=vec(mix_norm), w_in=mat(w_in), pool_w=mat(pool_w), pool_scale=vec(pool_scale),
        conv_dw=conv_dw, conv_b=vec(conv_b), conv_ln_g=vec(conv_ln_g), conv_ln_b=vec(conv_ln_b),
        w_br_attn=mat(w_br_attn), w_br_pool=mat(w_br_pool), w_br_conv=mat(w_br_conv), w_out=mat(w_out),
        xattn_norm=vec(xattn_norm), mem_norm=vec(mem_norm), xattn_wq=mat(xattn_wq),
        xattn_wkv=mat(xattn_wkv), xattn_wo=mat(xattn_wo),
        ffn2_norm=vec(ffn2_norm), ffn2_w_gu=mat(ffn2_w_gu), ffn2_w_down=mat(ffn2_w_down),
        final_norm=final_norm.reshape(1, -1),
    )
    biases = [_band_bias(rel_bias, g, r) for g, (_, r) in enumerate(DIL_PAIRS)]
    y_prompt = _trunk(x_prompt, mem_prompt, w, biases, depth)
    y_sample = _trunk(x_sample, mem_sample, w, biases, depth)
    return (y_prompt, y_sample)
```

```python
import functools
import math

import jax
import jax.numpy as jnp
from jax import lax
from jax.experimental import pallas as pl
from jax.experimental.pallas import tpu as pltpu

F32 = jnp.float32
BF16 = jnp.bfloat16

D_MODEL = 1024
D_FF = 2816
DIL_PAIRS = ((128, 1), (512, 4), (2048, 16))
HEADS_PER_GROUP = 4
N_HEADS_A = len(DIL_PAIRS) * HEADS_PER_GROUP
HEAD_DIM = 128
QKV_W = N_HEADS_A * HEAD_DIM
ATTN_W = HEADS_PER_GROUP * HEAD_DIM
BAND_HALF = 64
REL_BUCKETS = 32
REL_MAX_EXACT = 8
REL_MAX_DIST = 1024
POOL_SIZES = (2, 4, 8, 16)
POOL_GROUP = 128
POOL_W = len(POOL_SIZES) * POOL_GROUP
CONV_W = 512
CONV_K = 31
CONV_GROUPS = CONV_W // HEAD_DIM
X_HEADS = 4
X_W = X_HEADS * HEAD_DIM
PROJ_W = 3 * QKV_W + POOL_W + 2 * CONV_W
EPS = 1e-6
NEG_INF = -1e30
LOG2_E = math.log2(math.e)
LN_2 = math.log(2.0)

HALO = 16
CONV_CHUNK = 256
ROW_PHASES = 4
GATE_CHUNK = 256
Q_SUB = 128
ATTN_TILE = 2048
VMEM_LIMIT = 56 * 1024 * 1024


def _params(n_axes):
    return pltpu.CompilerParams(dimension_semantics=("arbitrary",) * n_axes,
                                vmem_limit_bytes=VMEM_LIMIT)


def _resident(shape, index):
    return pl.BlockSpec(shape, lambda *_: index, pipeline_mode=pl.Buffered(1))


def _rms(xf, g):
    ms = jnp.mean(xf * xf, axis=-1, keepdims=True)
    return xf * lax.rsqrt(ms + EPS) * g


def _sigmoid(x):
    return 1.0 / (1.0 + jnp.exp2(x * -LOG2_E))


def _zero_after(v):
    bits = pltpu.bitcast(v, jnp.uint32)
    return pltpu.bitcast((bits >> 16) >> 16, F32)


def _mem_attention(xf, g_ref, wq_ref, k_ref, v_ref, wo_ref, q_scr, o_scr):
    xn = _rms(xf, g_ref[0]).astype(BF16)
    q_scr[...] = jnp.dot(xn, wq_ref[0], preferred_element_type=F32).astype(BF16)
    scale2 = HEAD_DIM ** -0.5 * LOG2_E
    for h in range(X_HEADS):
        cols = slice(h * HEAD_DIM, (h + 1) * HEAD_DIM)
        s = lax.dot_general(q_scr[:, cols], k_ref[0, :, cols], (((1,), (1,)), ((), ())),
                            preferred_element_type=F32) * scale2
        p = jnp.exp2(s - jnp.max(s, axis=-1, keepdims=True))
        l = jnp.sum(p, axis=-1, keepdims=True)
        o = jnp.dot(p.astype(BF16), v_ref[0, :, cols], preferred_element_type=F32) / l
        o_scr[:, cols] = o.astype(BF16)
    return xf + jnp.dot(o_scr[...], wo_ref[0], preferred_element_type=F32)


def _ffn_body(x_ref, g_ref, wgu_ref, wd_ref, fg_ref, *rest, ck, final, with_mem):
    if with_mem:
        xg_ref, wq_ref, k_ref, v_ref, wo_ref, o_ref, xn_ref, act_ref, xmid_ref, q_scr, o_scr = rest
        xmid_ref[...] = _mem_attention(x_ref[0], xg_ref, wq_ref, k_ref, v_ref, wo_ref, q_scr, o_scr)
        residual = lambda: xmid_ref[...]
    else:
        o_ref, xn_ref, act_ref = rest
        residual = lambda: x_ref[0]
    xn_ref[...] = _rms(residual(), g_ref[0]).astype(BF16)
    for c in range(D_FF // ck):
        xn = xn_ref[...]
        a = jnp.dot(xn, wgu_ref[0, :, c * ck:(c + 1) * ck], preferred_element_type=F32)
        u = jnp.dot(xn, wgu_ref[0, :, D_FF + c * ck:D_FF + (c + 1) * ck], preferred_element_type=F32)
        act_ref[:, c * ck:(c + 1) * ck] = (a * _sigmoid(a) * u).astype(BF16)
    y = residual() + 0.5 * jnp.dot(act_ref[...], wd_ref[0], preferred_element_type=F32)
    if final:
        y = _rms(y, fg_ref[...])
    o_ref[0] = y


def _ffn(x, norm, w_gu, w_down, final_g, layer, *, final, mem=None, tm=1024, ck=256):
    b, s, d = x.shape
    tm = min(tm, s)
    body = functools.partial(_ffn_body, ck=ck, final=final, with_mem=mem is not None)
    in_specs = [
        pl.BlockSpec((1, tm, d), lambda i, j: (i, j, 0)),
        _resident((1, 1, d), (layer, 0, 0)),
        _resident((1, d, 2 * D_FF), (layer, 0, 0)),
        _resident((1, D_FF, d), (layer, 0, 0)),
        _resident((1, d), (0, 0)),
    ]
    scratch = [pltpu.VMEM((tm, d), BF16), pltpu.VMEM((tm, D_FF), BF16)]
    operands = [x, norm, w_gu, w_down, final_g]
    if mem is not None:
        xnorm, wq, kmem, vmem, wo = mem
        kv = pl.BlockSpec((1, kmem.shape[1], X_W), lambda i, j: (i, 0, 0))
        in_specs += [_resident((1, 1, d), (layer, 0, 0)), _resident((1, d, X_W), (layer, 0, 0)),
                     kv, kv, _resident((1, X_W, d), (layer, 0, 0))]
        scratch += [pltpu.VMEM((tm, d), F32), pltpu.VMEM((tm, X_W), BF16), pltpu.VMEM((tm, X_W), BF16)]
        operands += [xnorm, wq, kmem, vmem, wo]
    return pl.pallas_call(
        body,
        out_shape=jax.ShapeDtypeStruct(x.shape, F32),
        grid=(b, s // tm),
        in_specs=in_specs,
        out_specs=pl.BlockSpec((1, tm, d), lambda i, j: (i, j, 0)),
        scratch_shapes=scratch,
        compiler_params=_params(2),
        name="ffn_mem" if mem is not None else "ffn",
    )(*operands)


def _proj_body(x_ref, g_ref, w_ref, qkv0_ref, qkv1_ref, qkv2_ref, xp_ref, xc_ref, xno_ref,
               xn_ref, xn4_ref, xn16_ref, xs_ref):
    tm = xn_ref.shape[0]
    xn = _rms(x_ref[0], g_ref[0])
    xn_ref[...] = xn.astype(BF16)
    xno_ref[0] = xn_ref[...]

    def chunk(lhs_ref, base):
        return jnp.dot(lhs_ref[...], w_ref[0, :, base:base + ATTN_W], preferred_element_type=F32)

    for part in range(3):
        qkv0_ref[0, 0, :, part * ATTN_W:(part + 1) * ATTN_W] = chunk(xn_ref, part * QKV_W).astype(BF16)
    xp_ref[0] = chunk(xn_ref, 3 * QKV_W)
    xc_ref[0] = chunk(xn_ref, 3 * QKV_W + POOL_W) * _sigmoid(chunk(xn_ref, 3 * QKV_W + POOL_W + CONV_W))

    for t in range(xs_ref.shape[0]):
        xs_ref[t] = xn[:, t * HEAD_DIM:(t + 1) * HEAD_DIM]
    for g, (lhs_ref, out_ref) in ((1, (xn4_ref, qkv1_ref)), (2, (xn16_ref, qkv2_ref))):
        r = DIL_PAIRS[g][1]
        n = tm // r
        for c in range(r):
            for t in range(xs_ref.shape[0]):
                lhs_ref[c * n:(c + 1) * n, t * HEAD_DIM:(t + 1) * HEAD_DIM] = (
                    xs_ref[t, pl.ds(c, n, stride=r), :].astype(BF16))
        for part in range(3):
            y = chunk(lhs_ref, part * QKV_W + g * ATTN_W).astype(BF16)
            for c in range(r):
                out_ref[0, c, :, part * ATTN_W:(part + 1) * ATTN_W] = y[c * n:(c + 1) * n]


def _proj(x, norm, w_in, layer, *, tm=512):
    b, s, d = x.shape
    tm = min(tm, s)
    row = lambda w: pl.BlockSpec((1, tm, w), lambda i, j: (i, j, 0))
    rs = [r for _, r in DIL_PAIRS]
    qkv_shape = lambda r: jax.ShapeDtypeStruct((b, r, s // r, 3 * ATTN_W), BF16)
    qkv_spec = lambda r: pl.BlockSpec((1, r, tm // r, 3 * ATTN_W), lambda i, j: (i, 0, j, 0))
    return pl.pallas_call(
        _proj_body,
        out_shape=(*[qkv_shape(r) for r in rs],
                   jax.ShapeDtypeStruct((b, s, POOL_W), F32),
                   jax.ShapeDtypeStruct((b, s, CONV_W), F32),
                   jax.ShapeDtypeStruct((b, s, d), BF16)),
        grid=(b, s // tm),
        in_specs=[row(d), _resident((1, 1, d), (layer, 0, 0)),
                  _resident((1, d, PROJ_W), (layer, 0, 0))],
        out_specs=(*[qkv_spec(r) for r in rs], row(POOL_W), row(CONV_W), row(d)),
        scratch_shapes=[pltpu.VMEM((tm, d), BF16), pltpu.VMEM((tm, d), BF16), pltpu.VMEM((tm, d), BF16),
                        pltpu.VMEM((d // HEAD_DIM, tm, HEAD_DIM), F32)],
        compiler_params=_params(2),
        name="in_proj",
    )(x, norm, w_in)


def _attn_body(q_ref, kp_ref, kc_ref, kn_ref, vp_ref, vc_ref, vn_ref, bias_ref,
               o_ref, lse_ref, kedge, vedge, *, tl, seq_l):
    h64 = BAND_HALF
    n_sub = tl // Q_SUB
    win = 2 * Q_SUB
    row0 = pl.program_id(2) * tl
    scale2 = HEAD_DIM ** -0.5 * LOG2_E
    lane = lax.broadcasted_iota(jnp.int32, (Q_SUB, HEAD_DIM), 1)
    for cc in range(q_ref.shape[0]):
        for edge, prev, cur, nxt in ((kedge, kp_ref, kc_ref, kn_ref), (vedge, vp_ref, vc_ref, vn_ref)):
            if n_sub == 1:
                edge[cc, 0, 0:h64] = prev[cc]
                edge[cc, 0, h64:h64 + tl] = cur[cc]
                edge[cc, 0, h64 + tl:win] = nxt[cc]
            else:
                edge[cc, 0, 0:h64] = prev[cc]
                edge[cc, 0, h64:win] = cur[cc, 0:win - h64]
                edge[cc, 1, 0:win - h64] = cur[cc, tl - (win - h64):tl]
                edge[cc, 1, win - h64:win] = nxt[cc]

        def window(edge, cur, j, cols):
            if j == 0:
                return edge[cc, 0, :, cols]
            if j == n_sub - 1:
                return edge[cc, 1, :, cols]
            return cur[cc, j * Q_SUB - h64:j * Q_SUB - h64 + win, cols]

        for j in range(n_sub):
            rows = slice(j * Q_SUB, (j + 1) * Q_SUB)
            kmask = None
            if j == 0 or j == n_sub - 1:
                kpos = row0 + (j * Q_SUB - h64) + lax.broadcasted_iota(jnp.int32, (1, win), 1)
                kmask = jnp.where((kpos >= 0) & (kpos < seq_l), 0.0, NEG_INF).astype(F32)
            lse_blk = jnp.zeros((Q_SUB, HEAD_DIM), F32)
            for h in range(HEADS_PER_GROUP):
                cols = slice(h * HEAD_DIM, (h + 1) * HEAD_DIM)
                q = q_ref[cc, rows, cols]
                k = window(kedge, kc_ref, j, cols)
                v = window(vedge, vc_ref, j, cols)
                s = lax.dot_general(q, k, (((1,), (1,)), ((), ())), preferred_element_type=F32)
                s = s * scale2 + bias_ref[h]
                if kmask is not None:
                    s = s + kmask
                mx = jnp.max(s, axis=-1, keepdims=True)
                p = jnp.exp2(s - mx)
                l = jnp.sum(p, axis=-1, keepdims=True)
                o = jnp.dot(p.astype(BF16), v, preferred_element_type=F32) / l
                o_ref[cc, rows, cols] = o.astype(o_ref.dtype)
                lse_blk = jnp.where(lane == h, (mx + jnp.log2(l)) * LN_2, lse_blk)
            lse_ref[cc, rows, :] = lse_blk


def _attn_group(qkv, bias):
    b, r, sl, _ = qkv.shape
    tl = min(ATTN_TILE, sl)
    ncls = min(r, max(1, ATTN_TILE // sl))
    nh = sl // BAND_HALF
    per = tl // BAND_HALF
    cur = lambda part: pl.BlockSpec((None, ncls, tl, ATTN_W), lambda i, c, m: (i, c, m, part))
    prev = lambda part: pl.BlockSpec((None, ncls, BAND_HALF, ATTN_W),
                                     lambda i, c, m: (i, c, jnp.maximum(m * per - 1, 0), part))
    nxt = lambda part: pl.BlockSpec((None, ncls, BAND_HALF, ATTN_W),
                                    lambda i, c, m: (i, c, jnp.minimum((m + 1) * per, nh - 1), part))
    edge = pltpu.VMEM((ncls, 2, 2 * Q_SUB, ATTN_W), BF16)
    return pl.pallas_call(
        functools.partial(_attn_body, tl=tl, seq_l=sl),
        out_shape=(jax.ShapeDtypeStruct((b, r, sl, ATTN_W), BF16),
                   jax.ShapeDtypeStruct((b, r, sl, HEAD_DIM), F32)),
        grid=(b, r // ncls, sl // tl),
        in_specs=[cur(0), prev(1), cur(1), nxt(1), prev(2), cur(2), nxt(2),
                  _resident((HEADS_PER_GROUP, Q_SUB, 2 * Q_SUB), (0, 0, 0))],
        out_specs=(pl.BlockSpec((None, ncls, tl, ATTN_W), lambda i, c, m: (i, c, m, 0)),
                   pl.BlockSpec((None, ncls, tl, HEAD_DIM), lambda i, c, m: (i, c, m, 0))),
        scratch_shapes=[edge, edge],
        compiler_params=_params(3),
        name=f"dilated_attn_r{r}",
    )(qkv, qkv, qkv, qkv, qkv, qkv, qkv, bias)


def _t5_bucket(rel):
    half = REL_BUCKETS // 2
    n = jnp.abs(rel)
    nf = jnp.maximum(n, 1).astype(F32)
    large = REL_MAX_EXACT + (jnp.log(nf / REL_MAX_EXACT) / math.log(REL_MAX_DIST / REL_MAX_EXACT)
                             * (half - REL_MAX_EXACT)).astype(jnp.int32)
    large = jnp.minimum(large, half - 1)
    return jnp.where(rel > 0, half, 0) + jnp.where(n < REL_MAX_EXACT, n, large)


def _band_bias(rel_bias, g, r):
    delta = jnp.arange(2 * Q_SUB)[None, :] - BAND_HALF - jnp.arange(Q_SUB)[:, None]
    bucket = _t5_bucket(r * delta)
    table = rel_bias[:, g * HEADS_PER_GROUP:(g + 1) * HEADS_PER_GROUP].astype(F32)
    onehot = bucket[None, :, :, None] == jnp.arange(REL_BUCKETS)
    tbl = jnp.sum(jnp.where(onehot, table.T[:, None, None, :], 0.0), axis=-1)
    return jnp.where((jnp.abs(delta) <= BAND_HALF)[None], tbl * LOG2_E, NEG_INF)


def _pool_conv_steps(xp_p, xp_c, xp_n, xc_p, xc_c, xc_n, pw_ref, ps_ref, dw_ref, cb_ref, lg_ref, lb_ref,
                     yb_ref, yc_ref, pbuf, ubuf, sbuf, *, ts, seq, pace):
    i = pl.program_id(1)
    has_prev = (i > 0).astype(F32)
    has_next = (i < pl.num_programs(1) - 1).astype(F32)
    chunk = min(CONV_CHUNK, ts)
    quarter = chunk // ROW_PHASES
    steps = []

    def fill(buf, g, prev, cur, nxt):
        buf[g, 0:HALO] = prev * has_prev
        buf[g, HALO:HALO + ts] = cur
        buf[g, HALO + ts:HALO + ts + HALO] = nxt * has_next

    def rows_from(buf, g, start):
        return buf[g, pl.ds(HALO + start, quarter, stride=ROW_PHASES), :]

    def windowed(buf, g, row0, first, taps, weight):
        accs = [None] * ROW_PHASES
        for m in range(taps + ROW_PHASES - 1):
            v = rows_from(buf, g, row0 + first + m)
            for ph in range(ROW_PHASES):
                k = m - ph
                if 0 <= k < taps:
                    term = weight(k, v)
                    accs[ph] = term if accs[ph] is None else accs[ph] + term
        return accs

    def store_phases(g, row0, accs):
        for ph, acc in enumerate(accs):
            sbuf[g, pl.ds(row0 + ph, quarter, stride=ROW_PHASES), :] = acc

    def pool_group(g, kw):
        cols = slice(g * POOL_GROUP, (g + 1) * POOL_GROUP)
        fill(pbuf, g, xp_p[0, :, cols], xp_c[0, :, cols], xp_n[0, :, cols])
        for row0 in range(0, ts, chunk):
            store_phases(g, row0, windowed(pbuf, g, row0, -(kw // 2), kw, lambda k, v: v))
        pos = i * ts + lax.broadcasted_iota(jnp.int32, (ts, 1), 0)
        cnt = (jnp.minimum(pos + kw // 2, seq) - jnp.maximum(pos - kw // 2, 0)).astype(F32)
        pooled = (sbuf[g] / cnt - xp_c[0, :, cols]).astype(BF16)
        mixed = jnp.dot(pooled, pw_ref[0, g], preferred_element_type=F32)
        yb_ref[:, cols] = (mixed * ps_ref[0, :, cols]).astype(yb_ref.dtype)

    for g, kw in enumerate(POOL_SIZES):
        steps.append(functools.partial(pool_group, g, kw))

    def conv_fill(g):
        cols = slice(g * HEAD_DIM, (g + 1) * HEAD_DIM)
        fill(ubuf, g, xc_p[0, :, cols], xc_c[0, :, cols], xc_n[0, :, cols])

    def conv_pass(g, row0):
        cols = slice(g * HEAD_DIM, (g + 1) * HEAD_DIM)
        bias = cb_ref[0, :, cols]
        after = pace()
        if after is not None:
            bias = bias + _zero_after(after)
        accs = windowed(ubuf, g, row0, -(CONV_K // 2), CONV_K,
                        lambda k, v: dw_ref[0, k:k + 1, cols] * v)
        store_phases(g, row0, [acc + bias for acc in accs])

    def conv_norm():
        tot = sum(jnp.sum(sbuf[g], axis=-1, keepdims=True) for g in range(CONV_GROUPS))
        mu = tot / CONV_W
        sq = sum(jnp.sum((sbuf[g] - mu) ** 2, axis=-1, keepdims=True) for g in range(CONV_GROUPS))
        inv = lax.rsqrt(sq / CONV_W + EPS)
        for g in range(CONV_GROUPS):
            cols = slice(g * HEAD_DIM, (g + 1) * HEAD_DIM)
            y = (sbuf[g] - mu) * inv * lg_ref[0, :, cols] + lb_ref[0, :, cols]
            yc_ref[:, cols] = (y * _sigmoid(y)).astype(yc_ref.dtype)

    for g in range(CONV_GROUPS):
        steps.append(functools.partial(conv_fill, g))
        for row0 in range(0, ts, chunk):
            steps.append(functools.partial(conv_pass, g, row0))
    steps.append(conv_norm)
    return steps


def _emit_interleaved(first, second):
    order = sorted([((k + 1) / len(first), 0, k) for k in range(len(first))]
                   + [((k + 0.5) / len(second), 1, k) for k in range(len(second))])
    for _, which, k in order:
        (first, second)[which][k]()


def _merge_body(x_ref, xn_ref, o0, o1, o2, l0, l1, l2, xp_p, xp_c, xp_n, xc_p, xc_c, xc_n,
                pw_ref, ps_ref, dw_ref, cb_ref, lg_ref, lb_ref,
                wg_ref, wa_ref, wb_ref, wc_ref, wo_ref, out_ref,
                at_ref, yb_ref, yc_ref, gate_ref, obuf, lbuf, pbuf, ubuf, sbuf, *, seq):
    tm = at_ref.shape[0]
    d = D_MODEL
    n_g = len(DIL_PAIRS)

    def attn_head(h):
        for gi, (o, l, (_, r)) in enumerate(zip((o0, o1, o2), (l0, l1, l2), DIL_PAIRS)):
            for c in range(r):
                dst = pl.ds(c, tm // r, stride=r) if r > 1 else pl.ds(0, tm)
                if h == 0:
                    lbuf[gi, dst, :] = l[0, c]
                obuf[gi, h, dst, :] = o[0, c, :, h * HEAD_DIM:(h + 1) * HEAD_DIM].astype(F32)
        ls = [lbuf[gi, :, h:h + 1] for gi in range(n_g)]
        mx = jnp.maximum(jnp.maximum(ls[0], ls[1]), ls[2])
        es = [jnp.exp(t - mx) for t in ls]
        den = es[0] + es[1] + es[2]
        acc = sum((es[gi] / den) * obuf[gi, h] for gi in range(n_g))
        at_ref[:, h * HEAD_DIM:(h + 1) * HEAD_DIM] = acc.astype(BF16)

    last_gate = [None]

    def gate_chunk(bi, c):
        z = jnp.dot(xn_ref[0], wg_ref[0, :, bi * d + c * GATE_CHUNK:bi * d + (c + 1) * GATE_CHUNK],
                    preferred_element_type=F32)
        gate_ref[bi, :, c * GATE_CHUNK:(c + 1) * GATE_CHUNK] = z
        last_gate[0] = z[tm - 1:tm, GATE_CHUNK - HEAD_DIM:GATE_CHUNK]

    vpu_steps = [functools.partial(attn_head, h) for h in range(HEADS_PER_GROUP)]
    vpu_steps += _pool_conv_steps(xp_p, xp_c, xp_n, xc_p, xc_c, xc_n, pw_ref, ps_ref, dw_ref, cb_ref,
                                  lg_ref, lb_ref, yb_ref, yc_ref, pbuf, ubuf, sbuf, ts=tm, seq=seq,
                                  pace=lambda: last_gate[0])
    mxu_steps = [functools.partial(gate_chunk, bi, c) for bi in range(3) for c in range(d // GATE_CHUNK)]
    _emit_interleaved(vpu_steps, mxu_steps)

    merged = None
    for bi, (br, w) in enumerate(((at_ref, wa_ref), (yb_ref, wb_ref), (yc_ref, wc_ref))):
        y = _sigmoid(gate_ref[bi]) * jnp.dot(br[...], w[0], preferred_element_type=F32)
        merged = y if merged is None else merged + y
    out_ref[0] = x_ref[0] + jnp.dot(merged.astype(BF16), wo_ref[0], preferred_element_type=F32)


def _merge(x, xn, os_, lses, xp, xc, w, layer, *, tm=512):
    b, s, d = x.shape
    tm = min(tm, s)
    per = tm // HALO
    nh = s // HALO
    row = lambda wd: pl.BlockSpec((1, tm, wd), lambda i, j: (i, j, 0))
    cls = lambda r, wd: pl.BlockSpec((1, r, tm // r, wd), lambda i, j: (i, 0, j, 0))

    def halo_specs(wd):
        return (pl.BlockSpec((1, HALO, wd), lambda i, j: (i, jnp.maximum(j * per - 1, 0), 0)),
                row(wd),
                pl.BlockSpec((1, HALO, wd), lambda i, j: (i, jnp.minimum((j + 1) * per, nh - 1), 0)))

    rs = [r for _, r in DIL_PAIRS]
    vec = lambda wd: _resident((1, 1, wd), (layer, 0, 0))
    wbr = _resident((1, ATTN_W, d), (layer, 0, 0))
    tile_f32 = lambda groups, rows: pltpu.VMEM((groups, rows, HEAD_DIM), F32)
    return pl.pallas_call(
        functools.partial(_merge_body, seq=s),
        out_shape=jax.ShapeDtypeStruct(x.shape, F32),
        grid=(b, s // tm),
        in_specs=[row(d), row(d),
                  *[cls(r, ATTN_W) for r in rs], *[cls(r, HEAD_DIM) for r in rs],
                  *halo_specs(POOL_W), *halo_specs(CONV_W),
                  _resident((1, len(POOL_SIZES), POOL_GROUP, POOL_GROUP), (layer, 0, 0, 0)),
                  vec(POOL_W),
                  _resident((1, CONV_K, CONV_W), (layer, 0, 0)),
                  vec(CONV_W), vec(CONV_W), vec(CONV_W),
                  pl.BlockSpec((1, d, 3 * d), lambda i, j: (layer, 0, PROJ_W // (3 * d)),
                               pipeline_mode=pl.Buffered(1)),
                  wbr, wbr, wbr, _resident((1, d, d), (layer, 0, 0))],
        out_specs=row(d),
        scratch_shapes=[pltpu.VMEM((tm, ATTN_W), BF16), pltpu.VMEM((tm, POOL_W), BF16),
                        pltpu.VMEM((tm, CONV_W), BF16),
                        pltpu.VMEM((3, tm, d), F32),
                        pltpu.VMEM((len(rs), HEADS_PER_GROUP, tm, HEAD_DIM), F32),
                        tile_f32(len(rs), tm),
                        tile_f32(len(POOL_SIZES), tm + 2 * HALO),
                        tile_f32(CONV_GROUPS, tm + 2 * HALO),
                        tile_f32(CONV_GROUPS, tm)],
        compiler_params=_params(2),
        name="merge_out",
    )(x, xn, *os_, *lses, xp, xp, xp, xc, xc, xc,
      w["pool_w"], w["pool_scale"], w["conv_dw"], w["conv_b"], w["conv_ln_g"], w["conv_ln_b"],
      w["w_in"], w["w_br_attn"], w["w_br_pool"], w["w_br_conv"], w["w_out"])


def _memkv_body(m_ref, g_ref, w_ref, k_ref, v_ref):
    mn = _rms(m_ref[0], g_ref[0]).astype(BF16)
    kv = jnp.dot(mn, w_ref[0], preferred_element_type=F32)
    k_ref[0] = kv[:, :X_W].astype(BF16)
    v_ref[0] = kv[:, X_W:].astype(BF16)


def _memkv(mem, norm, wkv, layer):
    b, m, d = mem.shape
    out = pl.BlockSpec((1, m, X_W), lambda i: (i, 0, 0))
    return pl.pallas_call(
        _memkv_body,
        out_shape=(jax.ShapeDtypeStruct((b, m, X_W), BF16),) * 2,
        grid=(b,),
        in_specs=[pl.BlockSpec((1, m, d), lambda i: (i, 0, 0)),
                  _resident((1, 1, d), (layer, 0, 0)),
                  _resident((1, d, 2 * X_W), (layer, 0, 0))],
        out_specs=(out, out),
        compiler_params=_params(1),
        name="mem_kv",
    )(mem, norm, wkv)


def _trunk(x, mem, w, biases, depth):
    for l in range(depth):
        x = _ffn(x, w["ffn1_norm"], w["ffn1_w_gu"], w["ffn1_w_down"], w["final_norm"], l, final=False)
        *qkvs, xp, xc, xn = _proj(x, w["mix_norm"], w["w_in"], l)
        outs = [_attn_group(qkv, bias) for qkv, bias in zip(qkvs, biases)]
        x = _merge(x, xn, [o for o, _ in outs], [s for _, s in outs], xp, xc, w, l)
        kmem, vmem = _memkv(mem, w["mem_norm"], w["xattn_wkv"], l)
        x = _ffn(x, w["ffn2_norm"], w["ffn2_w_gu"], w["ffn2_w_down"], w["final_norm"], l,
                 final=(l == depth - 1),
                 mem=(w["xattn_norm"], w["xattn_wq"], kmem, vmem, w["xattn_wo"]))
    return x


def kernel(x_prompt, x_sample, mem_prompt, mem_sample, rel_bias, ffn1_norm, ffn1_w_gu, ffn1_w_down, mix_norm, w_in, pool_w, pool_scale, conv_dw, conv_b, conv_ln_g, conv_ln_b, w_br_attn, w_br_pool, w_br_conv, w_out, xattn_norm, mem_norm, xattn_wq, xattn_wkv, xattn_wo, ffn2_norm, ffn2_w_gu, ffn2_w_down, final_norm):
    depth = w_in.shape[0]
    mat = lambda t: t.astype(BF16)
    vec = lambda t: t.reshape(depth, 1, t.shape[-1])
    w = dict(
        ffn1_norm=vec(ffn1_norm), ffn1_w_gu=mat(ffn1_w_gu), ffn1_w_down=mat(ffn1_w_down),
        mix_norm=vec(mix_norm), w_in=mat(w_in), pool_w=mat(pool_w), pool_scale=vec(pool_scale),
        conv_dw=conv_dw, conv_b=vec(conv_b), conv_ln_g=vec(conv_ln_g), conv_ln_b=vec(conv_ln_b),
        w_br_attn=mat(w_br_attn), w_br_pool=mat(w_br_pool), w_br_conv=mat(w_br_conv), w_out=mat(w_out),
        xattn_norm=vec(xattn_norm), mem_norm=vec(mem_norm), xattn_wq=mat(xattn_wq),
        xattn_wkv=mat(xattn_wkv), xattn_wo=mat(xattn_wo),
        ffn2_norm=vec(ffn2_norm), ffn2_w_gu=mat(ffn2_w_gu), ffn2_w_down=mat(ffn2_w_down),
        final_norm=final_norm.reshape(1, -1),
    )
    biases = [_band_bias(rel_bias, g, r) for g, (_, r) in enumerate(DIL_PAIRS)]
    y_prompt = _trunk(x_prompt, mem_prompt, w, biases, depth)
    y_sample = _trunk(x_sample, mem_sample, w, biases, depth)
    return (y_prompt, y_sample)
```
